```python
import math
import jax, jax.numpy as jnp
from jax import lax
import numpy as np

D_MODEL = 2048
BATCH = 2
SEQ = 8192
DEPTH = 1

HEAD_DIM = 64
N_Q_HEADS = 16
N_KV_HEADS = 2
ATTN_WIDTH = N_Q_HEADS * HEAD_DIM
KV_WIDTH = N_KV_HEADS * HEAD_DIM
CONV_WIDTH = D_MODEL - ATTN_WIDTH
CONV_SIZE = 31
WINDOW = 128
Q_BLOCK = 128
ROPE_THETA = 10000.0
IN_WIDTH = ATTN_WIDTH + 2 * KV_WIDTH + 2 * CONV_WIDTH
N_GROUPS = 8
EXPERTS_PER_GROUP = 8
N_EXPERTS = N_GROUPS * EXPERTS_PER_GROUP
TOP_K = 2
EXPERT_FF = 512
MOE_BLOCK = 128
RMS_EPS = 1e-6
LN_EPS = 1e-5
NEG_INF = -1e30

kernel_name = "hymba_conformer_swa_sink_hmoe_adaln"


def rmsnorm(x, g):
    xf = x.astype(jnp.float32)
    y = xf * lax.rsqrt(jnp.mean(xf * xf, axis=-1, keepdims=True) + RMS_EPS)
    return (y * g.astype(jnp.float32)).astype(x.dtype)


def layernorm(x, g, b):
    xf = x.astype(jnp.float32)
    mu = jnp.mean(xf, axis=-1, keepdims=True)
    var = jnp.mean(jnp.square(xf - mu), axis=-1, keepdims=True)
    y = (xf - mu) * lax.rsqrt(var + LN_EPS)
    return (y * g.astype(jnp.float32) + b.astype(jnp.float32)).astype(x.dtype)


def modulate(h, shift, scale):
    return h * (1.0 + scale[:, None, :]) + shift[:, None, :]


def apply_rope(t, positions):
    half = HEAD_DIM // 2
    inv_freq = ROPE_THETA ** (-jnp.arange(half, dtype=jnp.float32) * 2.0 / HEAD_DIM)
    ang = positions.astype(jnp.float32)[..., None] * inv_freq
    cos = jnp.cos(ang)[:, :, None, :]
    sin = jnp.sin(ang)[:, :, None, :]
    tf = t.astype(jnp.float32)
    t1, t2 = tf[..., :half], tf[..., half:]
    out = jnp.concatenate([t1 * cos - t2 * sin, t2 * cos + t1 * sin], axis=-1)
    return out.astype(t.dtype)


def sliding_window_sink_attention(q, k, v, sinks):
    B, S, Hq, dh = q.shape
    G = Hq // N_KV_HEADS
    nb = S // Q_BLOCK
    qb = q.reshape(B, nb, Q_BLOCK, N_KV_HEADS, G, dh)
    kb = k.reshape(B, nb, Q_BLOCK, N_KV_HEADS, dh)
    vb = v.reshape(B, nb, Q_BLOCK, N_KV_HEADS, dh)
    pad = ((0, 0), (1, 0), (0, 0), (0, 0), (0, 0))
    kk = jnp.concatenate([jnp.pad(kb, pad)[:, :-1], kb], axis=2)
    vv = jnp.concatenate([jnp.pad(vb, pad)[:, :-1], vb], axis=2)
    s = jnp.einsum('bnqhgd,bnkhd->bnhgqk', qb, kk,
                   preferred_element_type=jnp.float32) * (1.0 / math.sqrt(dh))
    blk = jnp.arange(nb)[:, None]
    qpos = blk * Q_BLOCK + jnp.arange(Q_BLOCK)[None, :]
    kpos = (blk - 1) * Q_BLOCK + jnp.arange(2 * Q_BLOCK)[None, :]
    rel = qpos[:, :, None] - kpos[:, None, :]
    valid = (rel >= 0) & (rel < WINDOW) & (kpos[:, None, :] >= 0)
    s = jnp.where(valid[None, :, None, None], s, NEG_INF)
    sink = sinks.astype(jnp.float32).reshape(N_KV_HEADS, G)[None, None, :, :, None, None]
    m = jnp.maximum(jnp.max(s, axis=-1, keepdims=True), sink)
    p = jnp.exp(s - m)
    p = p / (jnp.sum(p, axis=-1, keepdims=True) + jnp.exp(sink - m))
    o = jnp.einsum('bnhgqk,bnkhd->bnqhgd', p.astype(v.dtype), vv)
    return o.reshape(B, S, Hq * dh)


def conformer_conv(a, b, conv_w, conv_b, ln_g, ln_b):
    u = a * jax.nn.sigmoid(b)
    C = u.shape[-1]
    u = jnp.pad(u, ((0, 0), (CONV_SIZE - 1, 0), (0, 0)))
    y = lax.conv_general_dilated(u, conv_w[:, None, :].astype(u.dtype), window_strides=(1,),
                                 padding='VALID', dimension_numbers=('NWC', 'WIO', 'NWC'),
                                 feature_group_count=C)
    y = y + conv_b
    y = layernorm(y, ln_g, ln_b)
    return jax.nn.silu(y)


def hierarchical_moe(h, w_group_router, b_group_router, w_expert_router, b_expert_router,
                     w_gate_up, w_down):
    B, S, D = h.shape
    T = B * S
    hf = h.reshape(T, D)
    hf32 = hf.astype(jnp.float32)
    g_logits = hf32 @ w_group_router.astype(jnp.float32) + b_group_router.astype(jnp.float32)
    g_prob = jax.nn.softmax(g_logits, axis=-1)
    g_sel = jnp.argmax(g_logits, axis=-1)
    p_g = jnp.take_along_axis(g_prob, g_sel[:, None], axis=1)
    e_logits = (hf32 @ w_expert_router.astype(jnp.float32)
                + b_expert_router.astype(jnp.float32)).reshape(T, N_GROUPS, EXPERTS_PER_GROUP)
    e_in_group = jnp.take_along_axis(e_logits, g_sel[:, None, None], axis=1)[:, 0]
    top_vals, top_loc = lax.top_k(e_in_group, TOP_K)
    weights = p_g * jax.nn.softmax(top_vals, axis=-1)
    expert_ids = g_sel[:, None] * EXPERTS_PER_GROUP + top_loc

    A = T * TOP_K
    flat_e = expert_ids.reshape(A).astype(jnp.int32)
    flat_tok = jnp.repeat(jnp.arange(T, dtype=jnp.int32), TOP_K)
    flat_w = weights.reshape(A)
    order = jnp.argsort(flat_e, stable=True)
    se, stok, sw = flat_e[order], flat_tok[order], flat_w[order]
    counts = jax.ops.segment_sum(jnp.ones((A,), jnp.int32), flat_e, num_segments=N_EXPERTS)
    padded = ((counts + MOE_BLOCK - 1) // MOE_BLOCK) * MOE_BLOCK
    pad_end = jnp.cumsum(padded)
    pad_start = pad_end - padded
    raw_start = jnp.cumsum(counts) - counts
    dest = pad_start[se] + (jnp.arange(A, dtype=jnp.int32) - raw_start[se])
    n_blocks = -(-(A + N_EXPERTS * (MOE_BLOCK - 1)) // MOE_BLOCK)
    rows = n_blocks * MOE_BLOCK
    row_tok = jnp.full((rows,), T, jnp.int32).at[dest].set(stok)
    row_w = jnp.zeros((rows,), flat_w.dtype).at[dest].set(sw)
    block_e = jnp.minimum(jnp.searchsorted(pad_end, jnp.arange(n_blocks) * MOE_BLOCK, side='right'),
                          N_EXPERTS - 1).astype(jnp.int32)
    x_ext = jnp.concatenate([hf, jnp.zeros((1, D), hf.dtype)], axis=0)
    xin = x_ext[row_tok].reshape(n_blocks, MOE_BLOCK, D)

    def expert_block(args):
        xb, e = args
        gu = xb @ w_gate_up[e]
        gate, up = gu[:, :EXPERT_FF], gu[:, EXPERT_FF:]
        return (jax.nn.silu(gate) * up) @ w_down[e]

    ys = lax.map(expert_block, (xin, block_e)).reshape(rows, D)
    out = jnp.zeros((T + 1, D), ys.dtype).at[row_tok].add(ys * row_w[:, None].astype(ys.dtype))
    return out[:T].reshape(B, S, D)


def setup_inputs(seed: int = 0) -> dict:
    key = jax.random.key(seed)
    ks = jax.random.split(key, 24)
    D = D_MODEL
    f32 = jnp.float32
    nrm = lambda k, shape, s: jax.random.normal(k, shape, f32) * s
    x = nrm(ks[0], (BATCH, SEQ, D), 1.0)
    c = nrm(ks[1], (BATCH, D), 1.0)
    offset = jax.random.randint(ks[2], (BATCH, 1), 0, 4096, dtype=jnp.int32)
    positions = jnp.arange(SEQ, dtype=jnp.int32)[None, :] + offset
    return {
        "x": x,
        "c": c,
        "positions": positions,
        "w_ada": nrm(ks[3], (D, 6 * D), D ** -0.5),
        "b_ada": nrm(ks[4], (6 * D,), 0.02),
        "g_mix": 1.0 + nrm(ks[5], (D,), 0.1),
        "w_in": nrm(ks[6], (D, IN_WIDTH), D ** -0.5),
        "b_in": nrm(ks[7], (IN_WIDTH,), 0.02),
        "attn_sinks": nrm(ks[8], (N_Q_HEADS,), 0.5),
        "conv_w": nrm(ks[9], (CONV_SIZE, CONV_WIDTH), CONV_SIZE ** -0.5),
        "conv_b": nrm(ks[10], (CONV_WIDTH,), 0.02),
        "conv_ln_g": 1.0 + nrm(ks[11], (CONV_WIDTH,), 0.1),
        "conv_ln_b": nrm(ks[12], (CONV_WIDTH,), 0.02),
        "w_out": nrm(ks[13], (D, D), D ** -0.5),
        "b_out": nrm(ks[14], (D,), 0.02),
        "g_ffn": 1.0 + nrm(ks[15], (D,), 0.1),
        "w_group_router": nrm(ks[16], (D, N_GROUPS), D ** -0.5),
        "b_group_router": nrm(ks[17], (N_GROUPS,), 0.01),
        "w_expert_router": nrm(ks[18], (D, N_EXPERTS), D ** -0.5),
        "b_expert_router": nrm(ks[19], (N_EXPERTS,), 0.01),
        "w_gate_up": nrm(ks[20], (N_EXPERTS, D, 2 * EXPERT_FF), D ** -0.5),
        "w_down": nrm(ks[21], (N_EXPERTS, EXPERT_FF, D), EXPERT_FF ** -0.5),
        "g_final": 1.0 + nrm(ks[22], (D,), 0.1),
    }


def reference(x, c, positions, w_ada, b_ada, g_mix, w_in, b_in, attn_sinks, conv_w, conv_b,
              conv_ln_g, conv_ln_b, w_out, b_out, g_ffn, w_group_router, b_group_router,
              w_expert_router, b_expert_router, w_gate_up, w_down, g_final):
    B, S, D = x.shape
    mod = jax.nn.silu(c) @ w_ada + b_ada
    shift1, scale1, gate1, shift2, scale2, gate2 = jnp.split(mod, 6, axis=-1)
    for _ in range(DEPTH):
        h = modulate(rmsnorm(x, g_mix), shift1, scale1)
        z = h @ w_in + b_in
        o = 0
        q = z[..., o:o + ATTN_WIDTH].reshape(B, S, N_Q_HEADS, HEAD_DIM); o += ATTN_WIDTH
        k = z[..., o:o + KV_WIDTH].reshape(B, S, N_KV_HEADS, HEAD_DIM); o += KV_WIDTH
        v = z[..., o:o + KV_WIDTH].reshape(B, S, N_KV_HEADS, HEAD_DIM); o += KV_WIDTH
        glu_a = z[..., o:o + CONV_WIDTH]; o += CONV_WIDTH
        glu_b = z[..., o:o + CONV_WIDTH]
        q = apply_rope(q, positions)
        k = apply_rope(k, positions)
        attn_out = sliding_window_sink_attention(q, k, v, attn_sinks)
        conv_out = conformer_conv(glu_a, glu_b, conv_w, conv_b, conv_ln_g, conv_ln_b)
        mixed = jnp.concatenate([attn_out, conv_out], axis=-1) @ w_out + b_out
        x = x + gate1[:, None, :] * mixed
        h2 = modulate(rmsnorm(x, g_ffn), shift2, scale2)
        ffn = hierarchical_moe(h2, w_group_router, b_group_router, w_expert_router,
                               b_expert_router, w_gate_up, w_down)
        x = x + gate2[:, None, :] * ffn
    return rmsnorm(x, g_final)
```

```python
import functools
import math

import jax
import jax.numpy as jnp
from jax import lax
from jax.experimental import pallas as pl
from jax.experimental.pallas import tpu as pltpu

F32 = jnp.float32
BF16 = jnp.bfloat16

HEAD_DIM = 64
N_Q_HEADS = 16
N_KV_HEADS = 2
ATTN_WIDTH = N_Q_HEADS * HEAD_DIM
KV_WIDTH = N_KV_HEADS * HEAD_DIM
CONV_SIZE = 31
WINDOW = 128
ROPE_THETA = 10000.0
N_GROUPS = 8
EXPERTS_PER_GROUP = 8
N_EXPERTS = N_GROUPS * EXPERTS_PER_GROUP
TOP_K = 2
EXPERT_FF = 512
RMS_EPS = 1e-6
LN_EPS = 1e-5
NEG_INF = -1e30

LANES = 128
SUBLANES = 8
VMEM_LIMIT = 56 * 1024 * 1024


def _params(semantics, vmem=VMEM_LIMIT):
    return pltpu.CompilerParams(dimension_semantics=semantics, vmem_limit_bytes=vmem)


def _ada_kernel(cb_ref, w_ref, b_ref, o_ref):
    nb, d = cb_ref.shape[0], w_ref.shape[0]
    nl = w_ref.shape[1] // LANES

    def body(k, acc):
        r = pl.multiple_of(k * SUBLANES, SUBLANES)
        w8 = w_ref[pl.ds(r, SUBLANES), :]
        new = []
        for b in range(nb):
            cv = cb_ref[b, pl.ds(r, SUBLANES), :]
            sv = cv * jax.nn.sigmoid(cv)
            for j in range(nl):
                new.append(acc[b * nl + j] + w8[:, j * LANES:(j + 1) * LANES] * sv)
        return tuple(new)

    init = tuple(jnp.zeros((SUBLANES, LANES), F32) for _ in range(nb * nl))
    acc = lax.fori_loop(0, d // SUBLANES, body, init)
    for b in range(nb):
        row = jnp.concatenate(
            [jnp.sum(acc[b * nl + j], axis=0, keepdims=True) for j in range(nl)], axis=1)
        o_ref[b:b + 1, :] = row + b_ref[...]


def _ada(c, w_ada, b_ada, tn=1024):
    nb, d = c.shape
    n = w_ada.shape[1]
    cb = jnp.broadcast_to(c[:, :, None], (nb, d, LANES))
    return pl.pallas_call(
        _ada_kernel,
        out_shape=jax.ShapeDtypeStruct((nb, n), F32),
        grid=(n // tn,),
        in_specs=[
            pl.BlockSpec((nb, d, LANES), lambda j: (0, 0, 0)),
            pl.BlockSpec((d, tn), lambda j: (0, j)),
            pl.BlockSpec((1, tn), lambda j: (0, j)),
        ],
        out_specs=pl.BlockSpec((nb, tn), lambda j: (0, j)),
        compiler_params=_params(("arbitrary",)),
        name="ada",
    )(cb, w_ada, b_ada.reshape(1, n))


def _rmsnorm_mod(x, g, shift, scale):
    ms = jnp.mean(x * x, axis=-1, keepdims=True)
    h = (x * lax.rsqrt(ms + RMS_EPS)) * g
    return h * (1.0 + scale) + shift


def _inproj_kernel(x_ref, pos_ref, g_ref, sh_ref, sc_ref, w_ref, b_ref, invf_ref, sgn_ref,
                   q_ref, kv_ref, ab_ref):
    h = _rmsnorm_mod(x_ref[...], g_ref[...], sh_ref[0], sc_ref[0]).astype(BF16)

    ang = pos_ref[...].astype(F32) * invf_ref[...]
    cosv = jnp.cos(ang)
    sinv = jnp.sin(ang) * sgn_ref[...]
    lane = lax.broadcasted_iota(jnp.int32, (1, LANES), 1)
    first_half = (lane % HEAD_DIM) < (HEAD_DIM // 2)

    def rope(t):
        rot = jnp.where(first_half, pltpu.roll(t, LANES - HEAD_DIM // 2, 1),
                        pltpu.roll(t, HEAD_DIM // 2, 1))
        return t * cosv + rot * sinv

    def proj(c0, width):
        return (jnp.dot(h, w_ref[:, c0:c0 + width], preferred_element_type=F32)
                + b_ref[:, c0:c0 + width])

    qscale = 1.0 / math.sqrt(HEAD_DIM)
    step = 512
    for c0 in range(0, ATTN_WIDTH, step):
        z = proj(c0, step)
        for j in range(step // LANES):
            q_ref[:, c0 + j * LANES:c0 + (j + 1) * LANES] = (
                rope(z[:, j * LANES:(j + 1) * LANES]) * qscale).astype(BF16)
    z = proj(ATTN_WIDTH, 2 * KV_WIDTH)
    k = rope(z[:, :KV_WIDTH])
    v = z[:, KV_WIDTH:]
    kv_ref[:, 0 * LANES:1 * LANES] = k.astype(BF16)
    kv_ref[:, 1 * LANES:2 * LANES] = pltpu.roll(k, HEAD_DIM, 1).astype(BF16)
    kv_ref[:, 2 * LANES:3 * LANES] = v.astype(BF16)
    kv_ref[:, 3 * LANES:4 * LANES] = pltpu.roll(v, HEAD_DIM, 1).astype(BF16)
    base = ATTN_WIDTH + 2 * KV_WIDTH
    for c0 in range(0, ab_ref.shape[1], step):
        ab_ref[:, c0:c0 + step] = proj(base + c0, step).astype(BF16)


def _inproj(x2d, pos2d, g_mix, shift1, scale1, w_in, b_in, seq, tm=512):
    t, d = x2d.shape
    n = w_in.shape[1]
    nb = shift1.shape[0]
    half = HEAD_DIM // 2
    inv_freq = ROPE_THETA ** (-jnp.arange(half, dtype=F32) * 2.0 / HEAD_DIM)
    invf = jnp.tile(inv_freq, LANES // half).reshape(1, LANES)
    sgn = jnp.tile(jnp.concatenate([-jnp.ones((half,), F32), jnp.ones((half,), F32)]),
                   LANES // HEAD_DIM).reshape(1, LANES)
    steps_per_seq = seq // tm
    conv_w2 = n - ATTN_WIDTH - 2 * KV_WIDTH
    bvec = lambda i: (i // steps_per_seq, 0, 0)
    return pl.pallas_call(
        _inproj_kernel,
        out_shape=(jax.ShapeDtypeStruct((t, ATTN_WIDTH), BF16),
                   jax.ShapeDtypeStruct((t, 4 * KV_WIDTH), BF16),
                   jax.ShapeDtypeStruct((t, conv_w2), BF16)),
        grid=(t // tm,),
        in_specs=[
            pl.BlockSpec((tm, d), lambda i: (i, 0)),
            pl.BlockSpec((tm, 1), lambda i: (i, 0)),
            pl.BlockSpec((1, d), lambda i: (0, 0)),
            pl.BlockSpec((1, 1, d), bvec),
            pl.BlockSpec((1, 1, d), bvec),
            pl.BlockSpec((d, n), lambda i: (0, 0)),
            pl.BlockSpec((1, n), lambda i: (0, 0)),
            pl.BlockSpec((1, LANES), lambda i: (0, 0)),
            pl.BlockSpec((1, LANES), lambda i: (0, 0)),
        ],
        out_specs=(pl.BlockSpec((tm, ATTN_WIDTH), lambda i: (i, 0)),
                   pl.BlockSpec((tm, 4 * KV_WIDTH), lambda i: (i, 0)),
                   pl.BlockSpec((tm, conv_w2), lambda i: (i, 0))),
        compiler_params=_params(("arbitrary",)),
        name="inproj",
    )(x2d, pos2d, g_mix.reshape(1, d), shift1.reshape(nb, 1, d), scale1.reshape(nb, 1, d),
      w_in.astype(BF16), b_in.reshape(1, n), invf, sgn)


def _attn_kernel(sink_ref, q_ref, kvm_ref, kvh_ref, o_ref):
    i = pl.program_id(1)
    tq = q_ref.shape[0]
    blk = WINDOW
    heads_per_group = N_Q_HEADS // N_KV_HEADS
    pairs = heads_per_group // 2
    row = lax.broadcasted_iota(jnp.int32, (blk, 2 * blk), 0)
    col = lax.broadcasted_iota(jnp.int32, (blk, 2 * blk), 1)
    rel = row + blk - col
    band = (rel >= 0) & (rel < WINDOW)
    lane = lax.broadcasted_iota(jnp.int32, (1, LANES), 1)
    lo = lane < HEAD_DIM
    zero = jnp.zeros((), BF16)

    for j in range(tq // blk):
        if j == 0:
            kv = jnp.concatenate([kvh_ref[...], kvm_ref[0:blk, :]], axis=0)
            valid = band & ((col >= blk) | (i > 0))
        else:
            kv = kvm_ref[(j - 1) * blk:(j + 1) * blk, :]
            valid = band
        k_nat, k_swp = kv[:, 0:LANES], kv[:, LANES:2 * LANES]
        v_nat, v_swp = kv[:, 2 * LANES:3 * LANES], kv[:, 3 * LANES:4 * LANES]
        for g in range(N_KV_HEADS):
            src_lo_k, src_hi_k = (k_nat, k_swp) if g == 0 else (k_swp, k_nat)
            src_lo_v, src_hi_v = (v_nat, v_swp) if g == 0 else (v_swp, v_nat)
            k_even = jnp.where(lo, src_lo_k, zero)
            k_odd = jnp.where(lo, zero, src_hi_k)
            v_even = jnp.where(lo, src_lo_v, zero)
            v_odd = jnp.where(lo, zero, src_hi_v)
            qs = jnp.concatenate(
                [q_ref[j * blk:(j + 1) * blk, (g * pairs + p) * LANES:(g * pairs + p + 1) * LANES]
                 for p in range(pairs)], axis=0)
            nt = (((1,), (1,)), ((), ()))
            s_even = lax.dot_general(qs, k_even, nt, preferred_element_type=F32)
            s_odd = lax.dot_general(qs, k_odd, nt, preferred_element_type=F32)
            probs, rinv = [], []
            for s_all, parity in ((s_even, 0), (s_odd, 1)):
                ps, rs = [], []
                for p in range(pairs):
                    sink = sink_ref[g * heads_per_group + 2 * p + parity]
                    s = jnp.where(valid, s_all[p * blk:(p + 1) * blk, :], NEG_INF)
                    m = jnp.maximum(jnp.max(s, axis=-1, keepdims=True), sink)
                    e = jnp.exp(s - m)
                    den = jnp.sum(e, axis=-1, keepdims=True) + jnp.exp(sink - m)
                    ps.append(e.astype(BF16))
                    rs.append(1.0 / den)
                probs.append(jnp.concatenate(ps, axis=0))
                rinv.append(rs)
            o = (jnp.dot(probs[0], v_even, preferred_element_type=F32)
                 + jnp.dot(probs[1], v_odd, preferred_element_type=F32))
            for p in range(pairs):
                scale = jnp.where(lo, rinv[0][p], rinv[1][p])
                c = (g * pairs + p) * LANES
                o_ref[j * blk:(j + 1) * blk, c:c + LANES] = (
                    o[p * blk:(p + 1) * blk, :] * scale).astype(BF16)


def _attn(q, kv, sinks, nb, seq, tq=512):
    blk = WINDOW
    q3 = q.reshape(nb, seq, ATTN_WIDTH)
    kv3 = kv.reshape(nb, seq, 4 * KV_WIDTH)
    sub = tq // blk
    out = pl.pallas_call(
        _attn_kernel,
        out_shape=jax.ShapeDtypeStruct((nb, seq, ATTN_WIDTH), BF16),
        grid=(nb, seq // tq),
        in_specs=[
            pl.BlockSpec(memory_space=pltpu.SMEM),
            pl.BlockSpec((None, tq, ATTN_WIDTH), lambda b, i: (b, i, 0)),
            pl.BlockSpec((None, tq, 4 * KV_WIDTH), lambda b, i: (b, i, 0)),
            pl.BlockSpec((None, blk, 4 * KV_WIDTH), lambda b, i: (b, jnp.maximum(i * sub - 1, 0), 0)),
        ],
        out_specs=pl.BlockSpec((None, tq, ATTN_WIDTH), lambda b, i: (b, i, 0)),
        compiler_params=_params(("arbitrary", "arbitrary")),
        name="attn",
    )(sinks.astype(F32), q3, kv3, kv3)
    return out.reshape(nb * seq, ATTN_WIDTH)


CONV_HALO = 32


def _mix_kernel(ab_ref, abh_ref, attn_ref, x_ref, wc_ref, cb_ref, lg_ref, lb_ref, wo_ref, bo_ref,
                gate_ref, o_ref, us_ref, y_ref):
    i = pl.program_id(1)
    tt = x_ref.shape[0]
    cw = y_ref.shape[1]

    def glu(ab):
        a = ab[:, :cw].astype(F32)
        b = ab[:, cw:].astype(F32)
        return a * jax.nn.sigmoid(b)

    halo = glu(abh_ref[...])
    us_ref[0, 0:CONV_HALO, :] = jnp.where(i > 0, halo, 0.0)
    us_ref[0, CONV_HALO:, :] = glu(ab_ref[...])
    n = tt + CONV_HALO - SUBLANES
    for s in range(1, SUBLANES):
        us_ref[s, 0:n, :] = us_ref[0, s:s + n, :]

    off = CONV_HALO - (CONV_SIZE - 1)

    def conv_lanes(c, carry):
        l0 = pl.multiple_of(c * LANES, LANES)
        acc = jnp.zeros((tt, LANES), F32)
        for k in range(CONV_SIZE):
            o = off + k
            a0 = (o // SUBLANES) * SUBLANES
            acc = acc + (us_ref[o % SUBLANES, a0:a0 + tt, pl.ds(l0, LANES)]
                         * wc_ref[k:k + 1, pl.ds(l0, LANES)])
        y_ref[:, pl.ds(l0, LANES)] = acc
        return carry

    lax.fori_loop(0, cw // LANES, conv_lanes, 0)

    y = y_ref[...] + cb_ref[...]
    mu = jnp.mean(y, axis=-1, keepdims=True)
    yc = y - mu
    var = jnp.mean(yc * yc, axis=-1, keepdims=True)
    yn = yc * lax.rsqrt(var + LN_EPS) * lg_ref[...] + lb_ref[...]
    conv = (yn * jax.nn.sigmoid(yn)).astype(BF16)

    aw = attn_ref.shape[1]
    mixed = (jnp.dot(attn_ref[...], wo_ref[0:aw, :], preferred_element_type=F32)
             + jnp.dot(conv, wo_ref[aw:, :], preferred_element_type=F32) + bo_ref[...])
    o_ref[...] = x_ref[...] + gate_ref[0] * mixed


def _mix(ab, attn, x2d, conv_w, conv_b, ln_g, ln_b, w_out, b_out, gate1, nb, seq, tt=256):
    t, d = x2d.shape
    cw = conv_w.shape[1]
    aw = attn.shape[1]
    ab3 = ab.reshape(nb, seq, 2 * cw)
    attn3 = attn.reshape(nb, seq, aw)
    x3 = x2d.reshape(nb, seq, d)
    hsub = tt // CONV_HALO
    out = pl.pallas_call(
        _mix_kernel,
        out_shape=jax.ShapeDtypeStruct((nb, seq, d), F32),
        grid=(nb, seq // tt),
        in_specs=[
            pl.BlockSpec((None, tt, 2 * cw), lambda b, i: (b, i, 0)),
            pl.BlockSpec((None, CONV_HALO, 2 * cw), lambda b, i: (b, jnp.maximum(i * hsub - 1, 0), 0)),
            pl.BlockSpec((None, tt, aw), lambda b, i: (b, i, 0)),
            pl.BlockSpec((None, tt, d), lambda b, i: (b, i, 0)),
            pl.BlockSpec((CONV_SIZE, cw), lambda b, i: (0, 0)),
            pl.BlockSpec((1, cw), lambda b, i: (0, 0)),
            pl.BlockSpec((1, cw), lambda b, i: (0, 0)),
            pl.BlockSpec((1, cw), lambda b, i: (0, 0)),
            pl.BlockSpec((d, d), lambda b, i: (0, 0)),
            pl.BlockSpec((1, d), lambda b, i: (0, 0)),
            pl.BlockSpec((1, 1, d), lambda b, i: (b, 0, 0)),
        ],
        out_specs=pl.BlockSpec((None, tt, d), lambda b, i: (b, i, 0)),
        scratch_shapes=[pltpu.VMEM((SUBLANES, CONV_HALO + tt, cw), F32), pltpu.VMEM((tt, cw), F32)],
        compiler_params=_params(("arbitrary", "arbitrary")),
        name="mix",
    )(ab3, ab3, attn3, x3, conv_w, conv_b.reshape(1, cw), ln_g.reshape(1, cw), ln_b.reshape(1, cw),
      w_out.astype(BF16), b_out.reshape(1, d), gate1.reshape(nb, 1, d))
    return out.reshape(t, d)


def _split_bf16(a):
    hi = a.astype(BF16)
    lo = (a - hi.astype(F32)).astype(BF16)
    return hi, lo


def _router_kernel(x_ref, g_ref, sh_ref, sc_ref, whi_ref, wlo_ref, b_ref, h_ref, r_ref):
    h = _rmsnorm_mod(x_ref[...], g_ref[...], sh_ref[0], sc_ref[0])
    h_ref[...] = h
    hi, lo = _split_bf16(h)
    logits = (jnp.dot(hi, whi_ref[...], preferred_element_type=F32)
              + jnp.dot(hi, wlo_ref[...], preferred_element_type=F32)
              + jnp.dot(lo, whi_ref[...], preferred_element_type=F32)) + b_ref[...]
    tm = logits.shape[0]
    lane = lax.broadcasted_iota(jnp.int32, (tm, LANES), 1)
    big = jnp.int32(LANES)

    def first_argmax(v):
        m = jnp.max(v, axis=-1, keepdims=True)
        idx = jnp.min(jnp.where(v == m, lane, big), axis=-1, keepdims=True)
        return m, idx

    is_group = lane < N_GROUPS
    gl = jnp.where(is_group, logits, NEG_INF)
    gmax, gsel = first_argmax(gl)
    p_g = 1.0 / jnp.sum(jnp.where(is_group, jnp.exp(gl - gmax), 0.0), axis=-1, keepdims=True)
    e_lo = N_GROUPS + gsel * EXPERTS_PER_GROUP
    in_group = (lane >= e_lo) & (lane < e_lo + EXPERTS_PER_GROUP)
    el = jnp.where(in_group, logits, NEG_INF)
    t1, i1 = first_argmax(el)
    el2 = jnp.where(lane == i1, NEG_INF, el)
    t2, i2 = first_argmax(el2)
    e2 = jnp.exp(t2 - t1)
    w1 = p_g / (1.0 + e2)
    w2 = p_g * e2 / (1.0 + e2)
    id1 = (i1 - N_GROUPS).astype(F32)
    id2 = (i2 - N_GROUPS).astype(F32)
    r_ref[...] = jnp.where(lane == 0, id1, jnp.where(lane == 1, id2,
                           jnp.where(lane == 2, w1, jnp.where(lane == 3, w2, 0.0))))


def _router(x2, g_ffn, shift2, scale2, w_gr, b_gr, w_er, b_er, seq, tm=512):
    t, d = x2.shape
    nb = shift2.shape[0]
    w = jnp.zeros((d, LANES), F32).at[:, :N_GROUPS].set(w_gr).at[:, N_GROUPS:N_GROUPS + N_EXPERTS].set(w_er)
    bias = jnp.zeros((1, LANES), F32).at[0, :N_GROUPS].set(b_gr).at[0, N_GROUPS:N_GROUPS + N_EXPERTS].set(b_er)
    whi = w.astype(BF16)
    wlo = (w - whi.astype(F32)).astype(BF16)
    steps_per_seq = seq // tm
    bvec = lambda i: (i // steps_per_seq, 0, 0)
    return pl.pallas_call(
        _router_kernel,
        out_shape=(jax.ShapeDtypeStruct((t, d), F32), jax.ShapeDtypeStruct((t, LANES), F32)),
        grid=(t // tm,),
        in_specs=[
            pl.BlockSpec((tm, d), lambda i: (i, 0)),
            pl.BlockSpec((1, d), lambda i: (0, 0)),
            pl.BlockSpec((1, 1, d), bvec),
            pl.BlockSpec((1, 1, d), bvec),
            pl.BlockSpec((d, LANES), lambda i: (0, 0)),
            pl.BlockSpec((d, LANES), lambda i: (0, 0)),
            pl.BlockSpec((1, LANES), lambda i: (0, 0)),
        ],
        out_specs=(pl.BlockSpec((tm, d), lambda i: (i, 0)), pl.BlockSpec((tm, LANES), lambda i: (i, 0))),
        compiler_params=_params(("arbitrary",)),
        name="router",
    )(x2, g_ffn.reshape(1, d), shift2.reshape(nb, 1, d), scale2.reshape(nb, 1, d), whi, wlo, bias)


def _final_kernel(x_ref, y_ref, r_ref, gate_ref, g_ref, o_ref):
    d = x_ref.shape[1]
    w0 = r_ref[:, 2:3]
    w1 = r_ref[:, 3:4]
    ffn = w0 * y_ref[:, 0:d] + w1 * y_ref[:, d:2 * d]
    x = x_ref[...] + gate_ref[0] * ffn
    ms = jnp.mean(x * x, axis=-1, keepdims=True)
    o_ref[...] = (x * lax.rsqrt(ms + RMS_EPS)) * g_ref[...]


def _final(x2, ys, route, gate2, g_final, seq, tm=512):
    t, d = x2.shape
    nb = gate2.shape[0]
    steps_per_seq = seq // tm
    return pl.pallas_call(
        _final_kernel,
        out_shape=jax.ShapeDtypeStruct((t, d), F32),
        grid=(t // tm,),
        in_specs=[
            pl.BlockSpec((tm, d), lambda i: (i, 0)),
            pl.BlockSpec((tm, TOP_K * d), lambda i: (i, 0)),
            pl.BlockSpec((tm, LANES), lambda i: (i, 0)),
            pl.BlockSpec((1, 1, d), lambda i: (i // steps_per_seq, 0, 0)),
            pl.BlockSpec((1, d), lambda i: (0, 0)),
        ],
        out_specs=pl.BlockSpec((tm, d), lambda i: (i, 0)),
        compiler_params=_params(("arbitrary",)),
        name="final",
    )(x2, ys, route, gate2.reshape(nb, 1, d), g_final.reshape(1, d))


MOE_ROWS = 256


def _moe_kernel(blk_e, blk_i0, blk_n, order, h_hbm, wgu_ref, wd_ref, ys_hbm, hbuf, obuf, gsem, ssem):
    b = pl.program_id(0)
    n = blk_n[b]
    tm = hbuf.shape[0]
    n_assign = order.shape[0]

    def row_copy_in(j, a):
        tok = a // TOP_K
        return pltpu.make_async_copy(h_hbm.at[pl.ds(tok, 1), :], hbuf.at[pl.ds(j, 1), :], gsem)

    def row_copy_out(j, dst):
        return pltpu.make_async_copy(obuf.at[pl.ds(j, 1), :], ys_hbm.at[pl.ds(dst, 1), :], ssem)

    @pl.when(b == 0)
    def _():
        obuf[...] = jnp.zeros_like(obuf)
        dump = pltpu.make_async_copy(obuf, ys_hbm.at[pl.ds(n_assign, tm), :], ssem)
        dump.start()
        dump.wait()

    @pl.when(n > 0)
    def _():
        i0 = blk_i0[b]
        last = i0 + n - 1

        def gather(j, carry):
            row_copy_in(j, order[jnp.minimum(i0 + j, last)]).start()
            return carry

        lax.fori_loop(0, tm, gather, 0)
        pltpu.make_async_copy(h_hbm.at[pl.ds(0, tm), :], hbuf, gsem).wait()

        h = hbuf[...].astype(BF16)
        gu = jnp.dot(h, wgu_ref[...].astype(BF16), preferred_element_type=F32)
        ff = gu.shape[1] // 2
        gate, up = gu[:, :ff], gu[:, ff:]
        act = (gate * jax.nn.sigmoid(gate) * up).astype(BF16)
        obuf[...] = jnp.dot(act, wd_ref[...].astype(BF16), preferred_element_type=F32)

        def scatter(j, carry):
            a = order[jnp.minimum(i0 + j, last)]
            dst = jnp.where(j < n, a, n_assign + j)
            row_copy_out(j, dst).start()
            return carry

        lax.fori_loop(0, tm, scatter, 0)
        pltpu.make_async_copy(obuf, ys_hbm.at[pl.ds(0, tm), :], ssem).wait()


def _moe_plan(route, tm):
    ids = route[:, 0:TOP_K].astype(jnp.int32)
    flat_e = ids.reshape(-1)
    n_assign = flat_e.shape[0]
    order = jnp.argsort(flat_e).astype(jnp.int32)
    counts = jnp.sum(flat_e[:, None] == jnp.arange(N_EXPERTS, dtype=jnp.int32)[None, :], axis=0,
                     dtype=jnp.int32)
    raw_start = jnp.cumsum(counts) - counts
    nblk = (counts + tm - 1) // tm
    blk_end = jnp.cumsum(nblk)
    n_blocks = -(-(n_assign + N_EXPERTS * (tm - 1)) // tm)
    bidx = jnp.arange(n_blocks, dtype=jnp.int32)
    total = blk_end[-1]
    e_of = jnp.searchsorted(blk_end, jnp.minimum(bidx, total - 1), side='right').astype(jnp.int32)
    e_of = jnp.minimum(e_of, N_EXPERTS - 1)
    j_in = bidx - (blk_end[e_of] - nblk[e_of])
    blk_i0 = raw_start[e_of] + j_in * tm
    blk_n = jnp.where(bidx < total, jnp.clip(counts[e_of] - j_in * tm, 0, tm), 0)
    return e_of, blk_i0.astype(jnp.int32), blk_n.astype(jnp.int32), order, n_blocks


def _moe(h2, route, w_gate_up, w_down, tm=MOE_ROWS):
    t, d = h2.shape
    ff2 = w_gate_up.shape[2]
    n_assign = t * TOP_K
    blk_e, blk_i0, blk_n, order, n_blocks = _moe_plan(route, tm)
    ys = pl.pallas_call(
        _moe_kernel,
        out_shape=jax.ShapeDtypeStruct((n_assign + tm, d), F32),
        grid_spec=pltpu.PrefetchScalarGridSpec(
            num_scalar_prefetch=4,
            grid=(n_blocks,),
            in_specs=[
                pl.BlockSpec(memory_space=pl.ANY),
                pl.BlockSpec((None, d, ff2), lambda b, be, bi, bn, od: (be[b], 0, 0)),
                pl.BlockSpec((None, ff2 // 2, d), lambda b, be, bi, bn, od: (be[b], 0, 0)),
            ],
            out_specs=pl.BlockSpec(memory_space=pl.ANY),
            scratch_shapes=[pltpu.VMEM((tm, d), F32), pltpu.VMEM((tm, d), F32),
                            pltpu.SemaphoreType.DMA, pltpu.SemaphoreType.DMA],
        ),
        compiler_params=_params(("arbitrary",)),
        name="moe",
    )(blk_e, blk_i0, blk_n, order, h2, w_gate_up, w_down)
    return ys.reshape((n_assign + tm) // TOP_K, TOP_K * d)


def kernel(x, c, positions, w_ada, b_ada, g_mix, w_in, b_in, attn_sinks, conv_w, conv_b, conv_ln_g,
           conv_ln_b, w_out, b_out, g_ffn, w_group_router, b_group_router, w_expert_router,
           b_expert_router, w_gate_up, w_down, g_final):
    nb, seq, d = x.shape
    t = nb * seq
    mod = _ada(c, w_ada, b_ada)
    shift1, scale1, gate1, shift2, scale2, gate2 = [mod[:, i * d:(i + 1) * d] for i in range(6)]
    x2d = x.reshape(t, d)
    q, kv, ab = _inproj(x2d, positions.reshape(t, 1), g_mix, shift1, scale1, w_in, b_in, seq)
    attn = _attn(q, kv, attn_sinks, nb, seq)
    x2 = _mix(ab, attn, x2d, conv_w, conv_b, conv_ln_g, conv_ln_b, w_out, b_out, gate1, nb, seq)
    h2, route = _router(x2, g_ffn, shift2, scale2, w_group_router, b_group_router,
                        w_expert_router, b_expert_router, seq)
    ys = _moe(h2, route, w_gate_up, w_down)
    out = _final(x2, ys, route, gate2, g_final, seq)
    return out.reshape(nb, seq, d)
```

```python
import functools
import math

import jax
import jax.numpy as jnp
from jax import lax
from jax.experimental import pallas as pl
from jax.experimental.pallas import tpu as pltpu

F32 = jnp.float32
BF16 = jnp.bfloat16

HEAD_DIM = 64
N_Q_HEADS = 16
N_KV_HEADS = 2
ATTN_WIDTH = N_Q_HEADS * HEAD_DIM
KV_WIDTH = N_KV_HEADS * HEAD_DIM
CONV_SIZE = 31
WINDOW = 128
ROPE_THETA = 10000.0
N_GROUPS = 8
EXPERTS_PER_GROUP = 8
N_EXPERTS = N_GROUPS * EXPERTS_PER_GROUP
TOP_K = 2
EXPERT_FF = 512
RMS_EPS = 1e-6
LN_EPS = 1e-5
NEG_INF = -1e30

LANES = 128
SUBLANES = 8
VMEM_LIMIT = 56 * 1024 * 1024


def _params(semantics, vmem=VMEM_LIMIT):
    return pltpu.CompilerParams(dimension_semantics=semantics, vmem_limit_bytes=vmem)


def _ada_kernel(cb_ref, w_ref, b_ref, o_ref):
    nb, d = cb_ref.shape[0], w_ref.shape[0]
    nl = w_ref.shape[1] // LANES

    def body(k, acc):
        r = pl.multiple_of(k * SUBLANES, SUBLANES)
        w8 = w_ref[pl.ds(r, SUBLANES), :]
        new = []
        for b in range(nb):
            cv = cb_ref[b, pl.ds(r, SUBLANES), :]
            sv = cv * jax.nn.sigmoid(cv)
            for j in range(nl):
                new.append(acc[b * nl + j] + w8[:, j * LANES:(j + 1) * LANES] * sv)
        return tuple(new)

    init = tuple(jnp.zeros((SUBLANES, LANES), F32) for _ in range(nb * nl))
    acc = lax.fori_loop(0, d // SUBLANES, body, init)
    for b in range(nb):
        row = jnp.concatenate(
            [jnp.sum(acc[b * nl + j], axis=0, keepdims=True) for j in range(nl)], axis=1)
        o_ref[b:b + 1, :] = row + b_ref[...]


def _ada(c, w_ada, b_ada, tn=1024):
    nb, d = c.shape
    n = w_ada.shape[1]
    cb = jnp.broadcast_to(c[:, :, None], (nb, d, LANES))
    return pl.pallas_call(
        _ada_kernel,
        out_shape=jax.ShapeDtypeStruct((nb, n), F32),
        grid=(n // tn,),
        in_specs=[
            pl.BlockSpec((nb, d, LANES), lambda j: (0, 0, 0)),
            pl.BlockSpec((d, tn), lambda j: (0, j)),
            pl.BlockSpec((1, tn), lambda j: (0, j)),
        ],
        out_specs=pl.BlockSpec((nb, tn), lambda j: (0, j)),
        compiler_params=_params(("arbitrary",)),
        name="ada",
    )(cb, w_ada, b_ada.reshape(1, n))


def _rmsnorm_mod(x, g, shift, scale):
    ms = jnp.mean(x * x, axis=-1, keepdims=True)
    h = (x * lax.rsqrt(ms + RMS_EPS)) * g
    return h * (1.0 + scale) + shift


def _inproj_kernel(x_ref, pos_ref, g_ref, sh_ref, sc_ref, w_ref, b_ref, invf_ref, sgn_ref,
                   q_ref, kv_ref, ab_ref):
    h = _rmsnorm_mod(x_ref[...], g_ref[...], sh_ref[0], sc_ref[0]).astype(BF16)

    ang = pos_ref[...].astype(F32) * invf_ref[...]
    cosv = jnp.cos(ang)
    sinv = jnp.sin(ang) * sgn_ref[...]
    lane = lax.broadcasted_iota(jnp.int32, (1, LANES), 1)
    first_half = (lane % HEAD_DIM) < (HEAD_DIM // 2)

    def rope(t):
        rot = jnp.where(first_half, pltpu.roll(t, LANES - HEAD_DIM // 2, 1),
                        pltpu.roll(t, HEAD_DIM // 2, 1))
        return t * cosv + rot * sinv

    def proj(c0, width):
        return (jnp.dot(h, w_ref[:, c0:c0 + width], preferred_element_type=F32)
                + b_ref[:, c0:c0 + width])

    qscale = 1.0 / math.sqrt(HEAD_DIM)
    step = 512
    for c0 in range(0, ATTN_WIDTH, step):
        z = proj(c0, step)
        for j in range(step // LANES):
            q_ref[:, c0 + j * LANES:c0 + (j + 1) * LANES] = (
                rope(z[:, j * LANES:(j + 1) * LANES]) * qscale).astype(BF16)
    z = proj(ATTN_WIDTH, 2 * KV_WIDTH)
    k = rope(z[:, :KV_WIDTH])
    v = z[:, KV_WIDTH:]
    kv_ref[:, 0 * LANES:1 * LANES] = k.astype(BF16)
    kv_ref[:, 1 * LANES:2 * LANES] = pltpu.roll(k, HEAD_DIM, 1).astype(BF16)
    kv_ref[:, 2 * LANES:3 * LANES] = v.astype(BF16)
    kv_ref[:, 3 * LANES:4 * LANES] = pltpu.roll(v, HEAD_DIM, 1).astype(BF16)
    base = ATTN_WIDTH + 2 * KV_WIDTH
    for c0 in range(0, ab_ref.shape[1], step):
        ab_ref[:, c0:c0 + step] = proj(base + c0, step).astype(BF16)


def _inproj(x2d, pos2d, g_mix, shift1, scale1, w_in, b_in, seq, tm=512):
    t, d = x2d.shape
    n = w_in.shape[1]
    nb = shift1.shape[0]
    half = HEAD_DIM // 2
    inv_freq = ROPE_THETA ** (-jnp.arange(half, dtype=F32) * 2.0 / HEAD_DIM)
    invf = jnp.tile(inv_freq, LANES // half).reshape(1, LANES)
    sgn = jnp.tile(jnp.concatenate([-jnp.ones((half,), F32), jnp.ones((half,), F32)]),
                   LANES // HEAD_DIM).reshape(1, LANES)
    steps_per_seq = seq // tm
    conv_w2 = n - ATTN_WIDTH - 2 * KV_WIDTH
    bvec = lambda i: (i // steps_per_seq, 0, 0)
    return pl.pallas_call(
        _inproj_kernel,
        out_shape=(jax.ShapeDtypeStruct((t, ATTN_WIDTH), BF16),
                   jax.ShapeDtypeStruct((t, 4 * KV_WIDTH), BF16),
                   jax.ShapeDtypeStruct((t, conv_w2), BF16)),
        grid=(t // tm,),
        in_specs=[
            pl.BlockSpec((tm, d), lambda i: (i, 0)),
            pl.BlockSpec((tm, 1), lambda i: (i, 0)),
            pl.BlockSpec((1, d), lambda i: (0, 0)),
            pl.BlockSpec((1, 1, d), bvec),
            pl.BlockSpec((1, 1, d), bvec),
            pl.BlockSpec((d, n), lambda i: (0, 0)),
            pl.BlockSpec((1, n), lambda i: (0, 0)),
            pl.BlockSpec((1, LANES), lambda i: (0, 0)),
            pl.BlockSpec((1, LANES), lambda i: (0, 0)),
        ],
        out_specs=(pl.BlockSpec((tm, ATTN_WIDTH), lambda i: (i, 0)),
                   pl.BlockSpec((tm, 4 * KV_WIDTH), lambda i: (i, 0)),
                   pl.BlockSpec((tm, conv_w2), lambda i: (i, 0))),
        compiler_params=_params(("arbitrary",)),
        name="inproj",
    )(x2d, pos2d, g_mix.reshape(1, d), shift1.reshape(nb, 1, d), scale1.reshape(nb, 1, d),
      w_in.astype(BF16), b_in.reshape(1, n), invf, sgn)


def _attn_kernel(sink_ref, q_ref, kvm_ref, kvh_ref, o_ref):
    i = pl.program_id(1)
    tq = q_ref.shape[0]
    blk = WINDOW
    heads_per_group = N_Q_HEADS // N_KV_HEADS
    pairs = heads_per_group // 2
    row = lax.broadcasted_iota(jnp.int32, (blk, 2 * blk), 0)
    col = lax.broadcasted_iota(jnp.int32, (blk, 2 * blk), 1)
    rel = row + blk - col
    band = (rel >= 0) & (rel < WINDOW)
    lane = lax.broadcasted_iota(jnp.int32, (1, LANES), 1)
    lo = lane < HEAD_DIM
    zero = jnp.zeros((), BF16)

    for j in range(tq // blk):
        if j == 0:
            kv = jnp.concatenate([kvh_ref[...], kvm_ref[0:blk, :]], axis=0)
            valid = band & ((col >= blk) | (i > 0))
        else:
            kv = kvm_ref[(j - 1) * blk:(j + 1) * blk, :]
            valid = band
        k_nat, k_swp = kv[:, 0:LANES], kv[:, LANES:2 * LANES]
        v_nat, v_swp = kv[:, 2 * LANES:3 * LANES], kv[:, 3 * LANES:4 * LANES]
        for g in range(N_KV_HEADS):
            src_lo_k, src_hi_k = (k_nat, k_swp) if g == 0 else (k_swp, k_nat)
            src_lo_v, src_hi_v = (v_nat, v_swp) if g == 0 else (v_swp, v_nat)
            k_even = jnp.where(lo, src_lo_k, zero)
            k_odd = jnp.where(lo, zero, src_hi_k)
            v_even = jnp.where(lo, src_lo_v, zero)
            v_odd = jnp.where(lo, zero, src_hi_v)
            qs = jnp.concatenate(
                [q_ref[j * blk:(j + 1) * blk, (g * pairs + p) * LANES:(g * pairs + p + 1) * LANES]
                 for p in range(pairs)], axis=0)
            nt = (((1,), (1,)), ((), ()))
            s_even = lax.dot_general(qs, k_even, nt, preferred_element_type=F32)
            s_odd = lax.dot_general(qs, k_odd, nt, preferred_element_type=F32)
            probs, rinv = [], []
            for s_all, parity in ((s_even, 0), (s_odd, 1)):
                ps, rs = [], []
                for p in range(pairs):
                    sink = sink_ref[g * heads_per_group + 2 * p + parity]
                    s = jnp.where(valid, s_all[p * blk:(p + 1) * blk, :], NEG_INF)
                    m = jnp.maximum(jnp.max(s, axis=-1, keepdims=True), sink)
                    e = jnp.exp(s - m)
                    den = jnp.sum(e, axis=-1, keepdims=True) + jnp.exp(sink - m)
                    ps.append(e.astype(BF16))
                    rs.append(1.0 / den)
                probs.append(jnp.concatenate(ps, axis=0))
                rinv.append(rs)
            o = (jnp.dot(probs[0], v_even, preferred_element_type=F32)
                 + jnp.dot(probs[1], v_odd, preferred_element_type=F32))
            for p in range(pairs):
                scale = jnp.where(lo, rinv[0][p], rinv[1][p])
                c = (g * pairs + p) * LANES
                o_ref[j * blk:(j + 1) * blk, c:c + LANES] = (
                    o[p * blk:(p + 1) * blk, :] * scale).astype(BF16)


def _attn(q, kv, sinks, nb, seq, tq=512):
    blk = WINDOW
    q3 = q.reshape(nb, seq, ATTN_WIDTH)
    kv3 = kv.reshape(nb, seq, 4 * KV_WIDTH)
    sub = tq // blk
    out = pl.pallas_call(
        _attn_kernel,
        out_shape=jax.ShapeDtypeStruct((nb, seq, ATTN_WIDTH), BF16),
        grid=(nb, seq // tq),
        in_specs=[
            pl.BlockSpec(memory_space=pltpu.SMEM),
            pl.BlockSpec((None, tq, ATTN_WIDTH), lambda b, i: (b, i, 0)),
            pl.BlockSpec((None, tq, 4 * KV_WIDTH), lambda b, i: (b, i, 0)),
            pl.BlockSpec((None, blk, 4 * KV_WIDTH), lambda b, i: (b, jnp.maximum(i * sub - 1, 0), 0)),
        ],
        out_specs=pl.BlockSpec((None, tq, ATTN_WIDTH), lambda b, i: (b, i, 0)),
        compiler_params=_params(("arbitrary", "arbitrary")),
        name="attn",
    )(sinks.astype(F32), q3, kv3, kv3)
    return out.reshape(nb * seq, ATTN_WIDTH)


CONV_HALO = 32


def _mix_kernel(ab_ref, abh_ref, attn_ref, x_ref, wc_ref, cb_ref, lg_ref, lb_ref, wo_ref, bo_ref,
                gate_ref, o_ref, us_ref, y_ref):
    i = pl.program_id(1)
    tt = x_ref.shape[0]
    cw = y_ref.shape[1]

    def glu(ab):
        a = ab[:, :cw].astype(F32)
        b = ab[:, cw:].astype(F32)
        return a * jax.nn.sigmoid(b)

    halo = glu(abh_ref[...])
    us_ref[0, 0:CONV_HALO, :] = jnp.where(i > 0, halo, 0.0)
    us_ref[0, CONV_HALO:, :] = glu(ab_ref[...])
    n = tt + CONV_HALO - SUBLANES
    for s in range(1, SUBLANES):
        us_ref[s, 0:n, :] = us_ref[0, s:s + n, :]

    off = CONV_HALO - (CONV_SIZE - 1)

    def conv_lanes(c, carry):
        l0 = pl.multiple_of(c * LANES, LANES)
        acc = jnp.zeros((tt, LANES), F32)
        for k in range(CONV_SIZE):
            o = off + k
            a0 = (o // SUBLANES) * SUBLANES
            acc = acc + (us_ref[o % SUBLANES, a0:a0 + tt, pl.ds(l0, LANES)]
                         * wc_ref[k:k + 1, pl.ds(l0, LANES)])
        y_ref[:, pl.ds(l0, LANES)] = acc
        return carry

    lax.fori_loop(0, cw // LANES, conv_lanes, 0)

    y = y_ref[...] + cb_ref[...]
    mu = jnp.mean(y, axis=-1, keepdims=True)
    yc = y - mu
    var = jnp.mean(yc * yc, axis=-1, keepdims=True)
    yn = yc * lax.rsqrt(var + LN_EPS) * lg_ref[...] + lb_ref[...]
    conv = (yn * jax.nn.sigmoid(yn)).astype(BF16)

    aw = attn_ref.shape[1]
    mixed = (jnp.dot(attn_ref[...], wo_ref[0:aw, :], preferred_element_type=F32)
             + jnp.dot(conv, wo_ref[aw:, :], preferred_element_type=F32) + bo_ref[...])
    o_ref[...] = x_ref[...] + gate_ref[0] * mixed


def _mix(ab, attn, x2d, conv_w, conv_b, ln_g, ln_b, w_out, b_out, gate1, nb, seq, tt=256):
    t, d = x2d.shape
    cw = conv_w.shape[1]
    aw = attn.shape[1]
    ab3 = ab.reshape(nb, seq, 2 * cw)
    attn3 = attn.reshape(nb, seq, aw)
    x3 = x2d.reshape(nb, seq, d)
    hsub = tt // CONV_HALO
    out = pl.pallas_call(
        _mix_kernel,
        out_shape=jax.ShapeDtypeStruct((nb, seq, d), F32),
        grid=(nb, seq // tt),
        in_specs=[
            pl.BlockSpec((None, tt, 2 * cw), lambda b, i: (b, i, 0)),
            pl.BlockSpec((None, CONV_HALO, 2 * cw), lambda b, i: (b, jnp.maximum(i * hsub - 1, 0), 0)),
            pl.BlockSpec((None, tt, aw), lambda b, i: (b, i, 0)),
            pl.BlockSpec((None, tt, d), lambda b, i: (b, i, 0)),
            pl.BlockSpec((CONV_SIZE, cw), lambda b, i: (0, 0)),
            pl.BlockSpec((1, cw), lambda b, i: (0, 0)),
            pl.BlockSpec((1, cw), lambda b, i: (0, 0)),
            pl.BlockSpec((1, cw), lambda b, i: (0, 0)),
            pl.BlockSpec((d, d), lambda b, i: (0, 0)),
            pl.BlockSpec((1, d), lambda b, i: (0, 0)),
            pl.BlockSpec((1, 1, d), lambda b, i: (b, 0, 0)),
        ],
        out_specs=pl.BlockSpec((None, tt, d), lambda b, i: (b, i, 0)),
        scratch_shapes=[pltpu.VMEM((SUBLANES, CONV_HALO + tt, cw), F32), pltpu.VMEM((tt, cw), F32)],
        compiler_params=_params(("arbitrary", "arbitrary")),
        name="mix",
    )(ab3, ab3, attn3, x3, conv_w, conv_b.reshape(1, cw), ln_g.reshape(1, cw), ln_b.reshape(1, cw),
      w_out.astype(BF16), b_out.reshape(1, d), gate1.reshape(nb, 1, d))
    return out.reshape(t, d)


def _split_bf16(a):
    hi = a.astype(BF16)
    lo = (a - hi.astype(F32)).astype(BF16)
    return hi, lo


def _router_kernel(x_ref, g_ref, sh_ref, sc_ref, whi_ref, wlo_ref, b_ref, h_ref, r_ref):
    h = _rmsnorm_mod(x_ref[...], g_ref[...], sh_ref[0], sc_ref[0])
    h_ref[...] = h
    hi, lo = _split_bf16(h)
    logits = (jnp.dot(hi, whi_ref[...], preferred_element_type=F32)
              + jnp.dot(hi, wlo_ref[...], preferred_element_type=F32)
              + jnp.dot(lo, whi_ref[...], preferred_element_type=F32)) + b_ref[...]
    tm = logits.shape[0]
    lane = lax.broadcasted_iota(jnp.int32, (tm, LANES), 1)
    big = jnp.int32(LANES)

    def first_argmax(v):
        m = jnp.max(v, axis=-1, keepdims=True)
        idx = jnp.min(jnp.where(v == m, lane, big), axis=-1, keepdims=True)
        return m, idx

    is_group = lane < N_GROUPS
    gl = jnp.where(is_group, logits, NEG_INF)
    gmax, gsel = first_argmax(gl)
    p_g = 1.0 / jnp.sum(jnp.where(is_group, jnp.exp(gl - gmax), 0.0), axis=-1, keepdims=True)
    e_lo = N_GROUPS + gsel * EXPERTS_PER_GROUP
    in_group = (lane >= e_lo) & (lane < e_lo + EXPERTS_PER_GROUP)
    el = jnp.where(in_group, logits, NEG_INF)
    t1, i1 = first_argmax(el)
    el2 = jnp.where(lane == i1, NEG_INF, el)
    t2, i2 = first_argmax(el2)
    e2 = jnp.exp(t2 - t1)
    w1 = p_g / (1.0 + e2)
    w2 = p_g * e2 / (1.0 + e2)
    id1 = (i1 - N_GROUPS).astype(F32)
    id2 = (i2 - N_GROUPS).astype(F32)
    r_ref[...] = jnp.where(lane == 0, id1, jnp.where(lane == 1, id2,
                           jnp.where(lane == 2, w1, jnp.where(lane == 3, w2, 0.0))))


def _router(x2, g_ffn, shift2, scale2, w_gr, b_gr, w_er, b_er, seq, tm=512):
    t, d = x2.shape
    nb = shift2.shape[0]
    w = jnp.zeros((d, LANES), F32).at[:, :N_GROUPS].set(w_gr).at[:, N_GROUPS:N_GROUPS + N_EXPERTS].set(w_er)
    bias = jnp.zeros((1, LANES), F32).at[0, :N_GROUPS].set(b_gr).at[0, N_GROUPS:N_GROUPS + N_EXPERTS].set(b_er)
    whi = w.astype(BF16)
    wlo = (w - whi.astype(F32)).astype(BF16)
    steps_per_seq = seq // tm
    bvec = lambda i: (i // steps_per_seq, 0, 0)
    return pl.pallas_call(
        _router_kernel,
        out_shape=(jax.ShapeDtypeStruct((t, d), F32), jax.ShapeDtypeStruct((t, LANES), F32)),
        grid=(t // tm,),
        in_specs=[
            pl.BlockSpec((tm, d), lambda i: (i, 0)),
            pl.BlockSpec((1, d), lambda i: (0, 0)),
            pl.BlockSpec((1, 1, d), bvec),
            pl.BlockSpec((1, 1, d), bvec),
            pl.BlockSpec((d, LANES), lambda i: (0, 0)),
            pl.BlockSpec((d, LANES), lambda i: (0, 0)),
            pl.BlockSpec((1, LANES), lambda i: (0, 0)),
        ],
        out_specs=(pl.BlockSpec((tm, d), lambda i: (i, 0)), pl.BlockSpec((tm, LANES), lambda i: (i, 0))),
        compiler_params=_params(("arbitrary",)),
        name="router",
    )(x2, g_ffn.reshape(1, d), shift2.reshape(nb, 1, d), scale2.reshape(nb, 1, d), whi, wlo, bias)


def _final_kernel(x_ref, y0_ref, y1_ref, r_ref, gate_ref, g_ref, o_ref):
    w0 = r_ref[:, 2:3]
    w1 = r_ref[:, 3:4]
    ffn = w0 * y0_ref[...] + w1 * y1_ref[...]
    x = x_ref[...] + gate_ref[0] * ffn
    ms = jnp.mean(x * x, axis=-1, keepdims=True)
    o_ref[...] = (x * lax.rsqrt(ms + RMS_EPS)) * g_ref[...]


def _final(x2, ys, route, gate2, g_final, seq, tm=512):
    t, d = x2.shape
    nb = gate2.shape[0]
    steps_per_seq = seq // tm
    return pl.pallas_call(
        _final_kernel,
        out_shape=jax.ShapeDtypeStruct((t, d), F32),
        grid=(t // tm,),
        in_specs=[
            pl.BlockSpec((tm, d), lambda i: (i, 0)),
            pl.BlockSpec((tm, d), lambda i: (i, 0)),
            pl.BlockSpec((tm, d), lambda i: (i + t // tm, 0)),
            pl.BlockSpec((tm, LANES), lambda i: (i, 0)),
            pl.BlockSpec((1, 1, d), lambda i: (i // steps_per_seq, 0, 0)),
            pl.BlockSpec((1, d), lambda i: (0, 0)),
        ],
        out_specs=pl.BlockSpec((tm, d), lambda i: (i, 0)),
        compiler_params=_params(("arbitrary",)),
        name="final",
    )(x2, ys, ys, route, gate2.reshape(nb, 1, d), g_final.reshape(1, d))


MOE_ROWS = 256
DMA_UNROLL = 8


def _moe_kernel(blk_e, blk_n, row_src, row_dst, h_hbm, wgu_ref, wd_ref, ys_hbm,
                hbuf, obuf, wgu_bf, wd_bf, gsem, ssem):
    b = pl.program_id(0)
    nblocks = pl.num_programs(0)
    n = blk_n[b]
    tm = hbuf.shape[1]
    dump0 = ys_hbm.shape[0] - 2 * tm
    slot = b % 2

    def gather_block(blk, s):
        base = blk * tm

        def body(j, carry):
            pltpu.make_async_copy(h_hbm.at[pl.ds(row_src[base + j], 1), :],
                                  hbuf.at[s, pl.ds(j, 1), :], gsem.at[s]).start()
            return carry

        lax.fori_loop(0, tm, body, 0, unroll=DMA_UNROLL)

    def scatter_block(blk, s):
        base = blk * tm

        def body(j, carry):
            pltpu.make_async_copy(obuf.at[s, pl.ds(j, 1), :],
                                  ys_hbm.at[pl.ds(row_dst[base + j], 1), :], ssem.at[s]).start()
            return carry

        lax.fori_loop(0, tm, body, 0, unroll=DMA_UNROLL)

    def wait_gather(s):
        pltpu.make_async_copy(h_hbm.at[pl.ds(0, tm), :], hbuf.at[s], gsem.at[s]).wait()

    def wait_scatter(s):
        pltpu.make_async_copy(obuf.at[s], ys_hbm.at[pl.ds(0, tm), :], ssem.at[s]).wait()

    @pl.when(b == 0)
    def _():
        obuf[0] = jnp.zeros(obuf.shape[1:], obuf.dtype)
        for s in range(2):
            dump = pltpu.make_async_copy(obuf.at[0], ys_hbm.at[pl.ds(dump0 + s * tm, tm), :], ssem.at[s])
            dump.start()
            dump.wait()
        gather_block(0, 0)

    @pl.when(n > 0)
    def _():
        nxt = jnp.minimum(b + 1, nblocks - 1)
        has_next = jnp.logical_and(b + 1 < nblocks, blk_n[nxt] > 0)

        @pl.when(has_next)
        def _():
            gather_block(b + 1, 1 - slot)

        prev_e = blk_e[jnp.maximum(b - 1, 0)]

        @pl.when(jnp.logical_or(b == 0, blk_e[b] != prev_e))
        def _():
            wgu_bf[...] = wgu_ref[...].astype(BF16)
            wd_bf[...] = wd_ref[...].astype(BF16)

        wait_gather(slot)

        @pl.when(b >= 2)
        def _():
            wait_scatter(slot)

        h = hbuf[slot].astype(BF16)
        gu = jnp.dot(h, wgu_bf[...], preferred_element_type=F32)
        ff = gu.shape[1] // 2
        gate, up = gu[:, :ff], gu[:, ff:]
        act = (gate * jax.nn.sigmoid(gate) * up).astype(BF16)
        obuf[slot] = jnp.dot(act, wd_bf[...], preferred_element_type=F32)
        scatter_block(b, slot)

        @pl.when(jnp.logical_not(has_next))
        def _():
            @pl.when(b >= 1)
            def _():
                wait_scatter(1 - slot)
            wait_scatter(slot)


def _moe_plan(route, tm, t):
    ids = route[:, 0:TOP_K].astype(jnp.int32)
    flat_e = ids.reshape(-1)
    n_assign = flat_e.shape[0]
    order = jnp.argsort(flat_e).astype(jnp.int32)
    counts = jnp.sum(flat_e[:, None] == jnp.arange(N_EXPERTS, dtype=jnp.int32)[None, :], axis=0,
                     dtype=jnp.int32)
    raw_start = jnp.cumsum(counts) - counts
    nblk = (counts + tm - 1) // tm
    blk_end = jnp.cumsum(nblk)
    n_blocks = -(-(n_assign + N_EXPERTS * (tm - 1)) // tm)
    bidx = jnp.arange(n_blocks, dtype=jnp.int32)
    total = blk_end[-1]
    e_of = jnp.searchsorted(blk_end, jnp.minimum(bidx, total - 1), side='right').astype(jnp.int32)
    e_of = jnp.minimum(e_of, N_EXPERTS - 1)
    j_in = bidx - (blk_end[e_of] - nblk[e_of])
    blk_i0 = raw_start[e_of] + j_in * tm
    blk_n = jnp.where(bidx < total, jnp.clip(counts[e_of] - j_in * tm, 0, tm), 0).astype(jnp.int32)
    j = jnp.arange(tm, dtype=jnp.int32)[None, :]
    last = blk_i0 + jnp.maximum(blk_n, 1) - 1
    idx = jnp.clip(jnp.minimum(blk_i0[:, None] + j, last[:, None]), 0, n_assign - 1)
    a = order[idx]
    tok, k = a // TOP_K, a % TOP_K
    row_src = tok
    row_dst = jnp.where(j < blk_n[:, None], k * t + tok, n_assign + (bidx[:, None] % 2) * tm + j)
    return e_of, blk_n, row_src.reshape(-1), row_dst.reshape(-1).astype(jnp.int32), n_blocks


def _moe(h2, route, w_gate_up, w_down, tm=MOE_ROWS):
    t, d = h2.shape
    ff2 = w_gate_up.shape[2]
    n_assign = t * TOP_K
    blk_e, blk_n, row_src, row_dst, n_blocks = _moe_plan(route, tm, t)
    return pl.pallas_call(
        _moe_kernel,
        out_shape=jax.ShapeDtypeStruct((n_assign + 2 * tm, d), F32),
        grid_spec=pltpu.PrefetchScalarGridSpec(
            num_scalar_prefetch=4,
            grid=(n_blocks,),
            in_specs=[
                pl.BlockSpec(memory_space=pl.ANY),
                pl.BlockSpec((None, d, ff2), lambda b, be, bn, rs, rd: (be[b], 0, 0)),
                pl.BlockSpec((None, ff2 // 2, d), lambda b, be, bn, rs, rd: (be[b], 0, 0)),
            ],
            out_specs=pl.BlockSpec(memory_space=pl.ANY),
            scratch_shapes=[pltpu.VMEM((2, tm, d), F32), pltpu.VMEM((2, tm, d), F32),
                            pltpu.VMEM((d, ff2), BF16), pltpu.VMEM((ff2 // 2, d), BF16),
                            pltpu.SemaphoreType.DMA((2,)), pltpu.SemaphoreType.DMA((2,))],
        ),
        compiler_params=_params(("arbitrary",)),
        name="moe",
    )(blk_e, blk_n, row_src, row_dst, h2, w_gate_up, w_down)


def kernel(x, c, positions, w_ada, b_ada, g_mix, w_in, b_in, attn_sinks, conv_w, conv_b, conv_ln_g,
           conv_ln_b, w_out, b_out, g_ffn, w_group_router, b_group_router, w_expert_router,
           b_expert_router, w_gate_up, w_down, g_final):
    nb, seq, d = x.shape
    t = nb * seq
    mod = _ada(c, w_ada, b_ada)
    shift1, scale1, gate1, shift2, scale2, gate2 = [mod[:, i * d:(i + 1) * d] for i in range(6)]
    x2d = x.reshape(t, d)
    q, kv, ab = _inproj(x2d, positions.reshape(t, 1), g_mix, shift1, scale1, w_in, b_in, seq)
    attn = _attn(q, kv, attn_sinks, nb, seq)
    x2 = _mix(ab, attn, x2d, conv_w, conv_b, conv_ln_g, conv_ln_b, w_out, b_out, gate1, nb, seq)
    h2, route = _router(x2, g_ffn, shift2, scale2, w_group_router, b_group_router,
                        w_expert_router, b_expert_router, seq)
    ys = _moe(h2, route, w_gate_up, w_down)
    out = _final(x2, ys, route, gate2, g_final, seq)
    return out.reshape(nb, seq, d)
```

```python
import functools
import math

import jax
import jax.numpy as jnp
from jax import lax
from jax.experimental import pallas as pl
from jax.experimental.pallas import tpu as pltpu

F32 = jnp.float32
BF16 = jnp.bfloat16

HEAD_DIM = 64
N_Q_HEADS = 16
N_KV_HEADS = 2
ATTN_WIDTH = N_Q_HEADS * HEAD_DIM
KV_WIDTH = N_KV_HEADS * HEAD_DIM
CONV_SIZE = 31
WINDOW = 128
ROPE_THETA = 10000.0
N_GROUPS = 8
EXPERTS_PER_GROUP = 8
N_EXPERTS = N_GROUPS * EXPERTS_PER_GROUP
TOP_K = 2
EXPERT_FF = 512
RMS_EPS = 1e-6
LN_EPS = 1e-5
NEG_INF = -1e30

LANES = 128
SUBLANES = 8
VMEM_LIMIT = 56 * 1024 * 1024


def _params(semantics, vmem=VMEM_LIMIT):
    return pltpu.CompilerParams(dimension_semantics=semantics, vmem_limit_bytes=vmem)


def _ada_kernel(cb_ref, w_ref, b_ref, o_ref):
    nb, d = cb_ref.shape[0], w_ref.shape[0]
    nl = w_ref.shape[1] // LANES

    def body(k, acc):
        r = pl.multiple_of(k * SUBLANES, SUBLANES)
        w8 = w_ref[pl.ds(r, SUBLANES), :]
        new = []
        for b in range(nb):
            cv = cb_ref[b, pl.ds(r, SUBLANES), :]
            sv = cv * jax.nn.sigmoid(cv)
            for j in range(nl):
                new.append(acc[b * nl + j] + w8[:, j * LANES:(j + 1) * LANES] * sv)
        return tuple(new)

    init = tuple(jnp.zeros((SUBLANES, LANES), F32) for _ in range(nb * nl))
    acc = lax.fori_loop(0, d // SUBLANES, body, init)
    for b in range(nb):
        row = jnp.concatenate(
            [jnp.sum(acc[b * nl + j], axis=0, keepdims=True) for j in range(nl)], axis=1)
        o_ref[b:b + 1, :] = row + b_ref[...]


def _ada(c, w_ada, b_ada, tn=1024):
    nb, d = c.shape
    n = w_ada.shape[1]
    cb = jnp.broadcast_to(c[:, :, None], (nb, d, LANES))
    return pl.pallas_call(
        _ada_kernel,
        out_shape=jax.ShapeDtypeStruct((nb, n), F32),
        grid=(n // tn,),
        in_specs=[
            pl.BlockSpec((nb, d, LANES), lambda j: (0, 0, 0)),
            pl.BlockSpec((d, tn), lambda j: (0, j)),
            pl.BlockSpec((1, tn), lambda j: (0, j)),
        ],
        out_specs=pl.BlockSpec((nb, tn), lambda j: (0, j)),
        compiler_params=_params(("arbitrary",)),
        name="ada",
    )(cb, w_ada, b_ada.reshape(1, n))


def _rmsnorm_mod(x, g, shift, scale):
    ms = jnp.mean(x * x, axis=-1, keepdims=True)
    h = (x * lax.rsqrt(ms + RMS_EPS)) * g
    return h * (1.0 + scale) + shift


def _inproj_kernel(x_ref, pos_ref, g_ref, sh_ref, sc_ref, w_ref, b_ref, invf_ref, sgn_ref,
                   q_ref, kv_ref, ab_ref):
    h = _rmsnorm_mod(x_ref[...], g_ref[...], sh_ref[0], sc_ref[0]).astype(BF16)

    ang = pos_ref[...].astype(F32) * invf_ref[...]
    cosv = jnp.cos(ang)
    sinv = jnp.sin(ang) * sgn_ref[...]
    lane = lax.broadcasted_iota(jnp.int32, (1, LANES), 1)
    first_half = (lane % HEAD_DIM) < (HEAD_DIM // 2)

    def rope(t):
        rot = jnp.where(first_half, pltpu.roll(t, LANES - HEAD_DIM // 2, 1),
                        pltpu.roll(t, HEAD_DIM // 2, 1))
        return t * cosv + rot * sinv

    def proj(c0, width):
        return (jnp.dot(h, w_ref[:, c0:c0 + width], preferred_element_type=F32)
                + b_ref[:, c0:c0 + width])

    qscale = 1.0 / math.sqrt(HEAD_DIM)
    step = 512
    for c0 in range(0, ATTN_WIDTH, step):
        z = proj(c0, step)
        for j in range(step // LANES):
            q_ref[:, c0 + j * LANES:c0 + (j + 1) * LANES] = (
                rope(z[:, j * LANES:(j + 1) * LANES]) * qscale).astype(BF16)
    z = proj(ATTN_WIDTH, 2 * KV_WIDTH)
    k = rope(z[:, :KV_WIDTH])
    v = z[:, KV_WIDTH:]
    kv_ref[:, 0 * LANES:1 * LANES] = k.astype(BF16)
    kv_ref[:, 1 * LANES:2 * LANES] = pltpu.roll(k, HEAD_DIM, 1).astype(BF16)
    kv_ref[:, 2 * LANES:3 * LANES] = v.astype(BF16)
    kv_ref[:, 3 * LANES:4 * LANES] = pltpu.roll(v, HEAD_DIM, 1).astype(BF16)
    base = ATTN_WIDTH + 2 * KV_WIDTH
    for c0 in range(0, ab_ref.shape[1], step):
        ab_ref[:, c0:c0 + step] = proj(base + c0, step).astype(BF16)


def _inproj(x2d, pos2d, g_mix, shift1, scale1, w_in, b_in, seq, tm=512):
    t, d = x2d.shape
    n = w_in.shape[1]
    nb = shift1.shape[0]
    half = HEAD_DIM // 2
    inv_freq = ROPE_THETA ** (-jnp.arange(half, dtype=F32) * 2.0 / HEAD_DIM)
    invf = jnp.tile(inv_freq, LANES // half).reshape(1, LANES)
    sgn = jnp.tile(jnp.concatenate([-jnp.ones((half,), F32), jnp.ones((half,), F32)]),
                   LANES // HEAD_DIM).reshape(1, LANES)
    steps_per_seq = seq // tm
    conv_w2 = n - ATTN_WIDTH - 2 * KV_WIDTH
    bvec = lambda i: (i // steps_per_seq, 0, 0)
    return pl.pallas_call(
        _inproj_kernel,
        out_shape=(jax.ShapeDtypeStruct((t, ATTN_WIDTH), BF16),
                   jax.ShapeDtypeStruct((t, 4 * KV_WIDTH), BF16),
                   jax.ShapeDtypeStruct((t, conv_w2), BF16)),
        grid=(t // tm,),
        in_specs=[
            pl.BlockSpec((tm, d), lambda i: (i, 0)),
            pl.BlockSpec((tm, 1), lambda i: (i, 0)),
            pl.BlockSpec((1, d), lambda i: (0, 0)),
            pl.BlockSpec((1, 1, d), bvec),
            pl.BlockSpec((1, 1, d), bvec),
            pl.BlockSpec((d, n), lambda i: (0, 0)),
            pl.BlockSpec((1, n), lambda i: (0, 0)),
            pl.BlockSpec((1, LANES), lambda i: (0, 0)),
            pl.BlockSpec((1, LANES), lambda i: (0, 0)),
        ],
        out_specs=(pl.BlockSpec((tm, ATTN_WIDTH), lambda i: (i, 0)),
                   pl.BlockSpec((tm, 4 * KV_WIDTH), lambda i: (i, 0)),
                   pl.BlockSpec((tm, conv_w2), lambda i: (i, 0))),
        compiler_params=_params(("arbitrary",)),
        name="inproj",
    )(x2d, pos2d, g_mix.reshape(1, d), shift1.reshape(nb, 1, d), scale1.reshape(nb, 1, d),
      w_in.astype(BF16), b_in.reshape(1, n), invf, sgn)


def _attn_kernel(sink_ref, q_ref, kvm_ref, kvh_ref, o_ref):
    i = pl.program_id(1)
    tq = q_ref.shape[0]
    blk = WINDOW
    heads_per_group = N_Q_HEADS // N_KV_HEADS
    pairs = heads_per_group // 2
    row = lax.broadcasted_iota(jnp.int32, (blk, 2 * blk), 0)
    col = lax.broadcasted_iota(jnp.int32, (blk, 2 * blk), 1)
    rel = row + blk - col
    band = (rel >= 0) & (rel < WINDOW)
    lane = lax.broadcasted_iota(jnp.int32, (1, LANES), 1)
    lo = lane < HEAD_DIM
    zero = jnp.zeros((), BF16)

    for j in range(tq // blk):
        if j == 0:
            kv = jnp.concatenate([kvh_ref[...], kvm_ref[0:blk, :]], axis=0)
            valid = band & ((col >= blk) | (i > 0))
        else:
            kv = kvm_ref[(j - 1) * blk:(j + 1) * blk, :]
            valid = band
        k_nat, k_swp = kv[:, 0:LANES], kv[:, LANES:2 * LANES]
        v_nat, v_swp = kv[:, 2 * LANES:3 * LANES], kv[:, 3 * LANES:4 * LANES]
        for g in range(N_KV_HEADS):
            src_lo_k, src_hi_k = (k_nat, k_swp) if g == 0 else (k_swp, k_nat)
            src_lo_v, src_hi_v = (v_nat, v_swp) if g == 0 else (v_swp, v_nat)
            k_even = jnp.where(lo, src_lo_k, zero)
            k_odd = jnp.where(lo, zero, src_hi_k)
            v_even = jnp.where(lo, src_lo_v, zero)
            v_odd = jnp.where(lo, zero, src_hi_v)
            qs = jnp.concatenate(
                [q_ref[j * blk:(j + 1) * blk, (g * pairs + p) * LANES:(g * pairs + p + 1) * LANES]
                 for p in range(pairs)], axis=0)
            nt = (((1,), (1,)), ((), ()))
            s_even = lax.dot_general(qs, k_even, nt, preferred_element_type=F32)
            s_odd = lax.dot_general(qs, k_odd, nt, preferred_element_type=F32)
            probs, rinv = [], []
            for s_all, parity in ((s_even, 0), (s_odd, 1)):
                ps, rs = [], []
                for p in range(pairs):
                    sink = sink_ref[g * heads_per_group + 2 * p + parity]
                    s = jnp.where(valid, s_all[p * blk:(p + 1) * blk, :], NEG_INF)
                    m = jnp.maximum(jnp.max(s, axis=-1, keepdims=True), sink)
                    e = jnp.exp(s - m)
                    den = jnp.sum(e, axis=-1, keepdims=True) + jnp.exp(sink - m)
                    ps.append(e.astype(BF16))
                    rs.append(1.0 / den)
                probs.append(jnp.concatenate(ps, axis=0))
                rinv.append(rs)
            o = (jnp.dot(probs[0], v_even, preferred_element_type=F32)
                 + jnp.dot(probs[1], v_odd, preferred_element_type=F32))
            for p in range(pairs):
                scale = jnp.where(lo, rinv[0][p], rinv[1][p])
                c = (g * pairs + p) * LANES
                o_ref[j * blk:(j + 1) * blk, c:c + LANES] = (
                    o[p * blk:(p + 1) * blk, :] * scale).astype(BF16)


def _attn(q, kv, sinks, nb, seq, tq=512):
    blk = WINDOW
    q3 = q.reshape(nb, seq, ATTN_WIDTH)
    kv3 = kv.reshape(nb, seq, 4 * KV_WIDTH)
    sub = tq // blk
    out = pl.pallas_call(
        _attn_kernel,
        out_shape=jax.ShapeDtypeStruct((nb, seq, ATTN_WIDTH), BF16),
        grid=(nb, seq // tq),
        in_specs=[
            pl.BlockSpec(memory_space=pltpu.SMEM),
            pl.BlockSpec((None, tq, ATTN_WIDTH), lambda b, i: (b, i, 0)),
            pl.BlockSpec((None, tq, 4 * KV_WIDTH), lambda b, i: (b, i, 0)),
            pl.BlockSpec((None, blk, 4 * KV_WIDTH), lambda b, i: (b, jnp.maximum(i * sub - 1, 0), 0)),
        ],
        out_specs=pl.BlockSpec((None, tq, ATTN_WIDTH), lambda b, i: (b, i, 0)),
        compiler_params=_params(("arbitrary", "arbitrary")),
        name="attn",
    )(sinks.astype(F32), q3, kv3, kv3)
    return out.reshape(nb * seq, ATTN_WIDTH)


CONV_HALO = 32


def _mix_kernel(ab_ref, abh_ref, attn_ref, x_ref, wc_ref, cb_ref, lg_ref, lb_ref, wo_ref, bo_ref,
                gate_ref, o_ref, us_ref, y_ref):
    i = pl.program_id(1)
    tt = x_ref.shape[0]
    cw = y_ref.shape[1]

    def glu(ab):
        a = ab[:, :cw].astype(F32)
        b = ab[:, cw:].astype(F32)
        return a * jax.nn.sigmoid(b)

    halo = glu(abh_ref[...])
    us_ref[0, 0:CONV_HALO, :] = jnp.where(i > 0, halo, 0.0)
    us_ref[0, CONV_HALO:, :] = glu(ab_ref[...])
    n = tt + CONV_HALO - SUBLANES
    for s in range(1, SUBLANES):
        us_ref[s, 0:n, :] = us_ref[0, s:s + n, :]

    off = CONV_HALO - (CONV_SIZE - 1)

    def conv_lanes(c, carry):
        l0 = pl.multiple_of(c * LANES, LANES)
        acc = jnp.zeros((tt, LANES), F32)
        for k in range(CONV_SIZE):
            o = off + k
            a0 = (o // SUBLANES) * SUBLANES
            acc = acc + (us_ref[o % SUBLANES, a0:a0 + tt, pl.ds(l0, LANES)]
                         * wc_ref[k:k + 1, pl.ds(l0, LANES)])
        y_ref[:, pl.ds(l0, LANES)] = acc
        return carry

    lax.fori_loop(0, cw // LANES, conv_lanes, 0)

    y = y_ref[...] + cb_ref[...]
    mu = jnp.mean(y, axis=-1, keepdims=True)
    yc = y - mu
    var = jnp.mean(yc * yc, axis=-1, keepdims=True)
    yn = yc * lax.rsqrt(var + LN_EPS) * lg_ref[...] + lb_ref[...]
    conv = (yn * jax.nn.sigmoid(yn)).astype(BF16)

    aw = attn_ref.shape[1]
    mixed = (jnp.dot(attn_ref[...], wo_ref[0:aw, :], preferred_element_type=F32)
             + jnp.dot(conv, wo_ref[aw:, :], preferred_element_type=F32) + bo_ref[...])
    o_ref[...] = x_ref[...] + gate_ref[0] * mixed


def _mix(ab, attn, x2d, conv_w, conv_b, ln_g, ln_b, w_out, b_out, gate1, nb, seq, tt=256):
    t, d = x2d.shape
    cw = conv_w.shape[1]
    aw = attn.shape[1]
    ab3 = ab.reshape(nb, seq, 2 * cw)
    attn3 = attn.reshape(nb, seq, aw)
    x3 = x2d.reshape(nb, seq, d)
    hsub = tt // CONV_HALO
    out = pl.pallas_call(
        _mix_kernel,
        out_shape=jax.ShapeDtypeStruct((nb, seq, d), F32),
        grid=(nb, seq // tt),
        in_specs=[
            pl.BlockSpec((None, tt, 2 * cw), lambda b, i: (b, i, 0)),
            pl.BlockSpec((None, CONV_HALO, 2 * cw), lambda b, i: (b, jnp.maximum(i * hsub - 1, 0), 0)),
            pl.BlockSpec((None, tt, aw), lambda b, i: (b, i, 0)),
            pl.BlockSpec((None, tt, d), lambda b, i: (b, i, 0)),
            pl.BlockSpec((CONV_SIZE, cw), lambda b, i: (0, 0)),
            pl.BlockSpec((1, cw), lambda b, i: (0, 0)),
            pl.BlockSpec((1, cw), lambda b, i: (0, 0)),
            pl.BlockSpec((1, cw), lambda b, i: (0, 0)),
            pl.BlockSpec((d, d), lambda b, i: (0, 0)),
            pl.BlockSpec((1, d), lambda b, i: (0, 0)),
            pl.BlockSpec((1, 1, d), lambda b, i: (b, 0, 0)),
        ],
        out_specs=pl.BlockSpec((None, tt, d), lambda b, i: (b, i, 0)),
        scratch_shapes=[pltpu.VMEM((SUBLANES, CONV_HALO + tt, cw), F32), pltpu.VMEM((tt, cw), F32)],
        compiler_params=_params(("arbitrary", "arbitrary")),
        name="mix",
    )(ab3, ab3, attn3, x3, conv_w, conv_b.reshape(1, cw), ln_g.reshape(1, cw), ln_b.reshape(1, cw),
      w_out.astype(BF16), b_out.reshape(1, d), gate1.reshape(nb, 1, d))
    return out.reshape(t, d)


def _split_bf16(a):
    hi = a.astype(BF16)
    lo = (a - hi.astype(F32)).astype(BF16)
    return hi, lo


def _router_kernel(x_ref, g_ref, sh_ref, sc_ref, whi_ref, wlo_ref, b_ref, h_ref, r_ref):
    h = _rmsnorm_mod(x_ref[...], g_ref[...], sh_ref[0], sc_ref[0])
    h_ref[...] = h
    hi, lo = _split_bf16(h)
    logits = (jnp.dot(hi, whi_ref[...], preferred_element_type=F32)
              + jnp.dot(hi, wlo_ref[...], preferred_element_type=F32)
              + jnp.dot(lo, whi_ref[...], preferred_element_type=F32)) + b_ref[...]
    tm = logits.shape[0]
    lane = lax.broadcasted_iota(jnp.int32, (tm, LANES), 1)
    big = jnp.int32(LANES)

    def first_argmax(v):
        m = jnp.max(v, axis=-1, keepdims=True)
        idx = jnp.min(jnp.where(v == m, lane, big), axis=-1, keepdims=True)
        return m, idx

    is_group = lane < N_GROUPS
    gl = jnp.where(is_group, logits, NEG_INF)
    gmax, gsel = first_argmax(gl)
    p_g = 1.0 / jnp.sum(jnp.where(is_group, jnp.exp(gl - gmax), 0.0), axis=-1, keepdims=True)
    e_lo = N_GROUPS + gsel * EXPERTS_PER_GROUP
    in_group = (lane >= e_lo) & (lane < e_lo + EXPERTS_PER_GROUP)
    el = jnp.where(in_group, logits, NEG_INF)
    t1, i1 = first_argmax(el)
    el2 = jnp.where(lane == i1, NEG_INF, el)
    t2, i2 = first_argmax(el2)
    e2 = jnp.exp(t2 - t1)
    w1 = p_g / (1.0 + e2)
    w2 = p_g * e2 / (1.0 + e2)
    id1 = (i1 - N_GROUPS).astype(F32)
    id2 = (i2 - N_GROUPS).astype(F32)
    r_ref[...] = jnp.where(lane == 0, id1, jnp.where(lane == 1, id2,
                           jnp.where(lane == 2, w1, jnp.where(lane == 3, w2, 0.0))))


def _router(x2, g_ffn, shift2, scale2, w_gr, b_gr, w_er, b_er, seq, tm=512):
    t, d = x2.shape
    nb = shift2.shape[0]
    w = jnp.zeros((d, LANES), F32).at[:, :N_GROUPS].set(w_gr).at[:, N_GROUPS:N_GROUPS + N_EXPERTS].set(w_er)
    bias = jnp.zeros((1, LANES), F32).at[0, :N_GROUPS].set(b_gr).at[0, N_GROUPS:N_GROUPS + N_EXPERTS].set(b_er)
    whi = w.astype(BF16)
    wlo = (w - whi.astype(F32)).astype(BF16)
    steps_per_seq = seq // tm
    bvec = lambda i: (i // steps_per_seq, 0, 0)
    return pl.pallas_call(
        _router_kernel,
        out_shape=(jax.ShapeDtypeStruct((t, d), F32), jax.ShapeDtypeStruct((t, LANES), F32)),
        grid=(t // tm,),
        in_specs=[
            pl.BlockSpec((tm, d), lambda i: (i, 0)),
            pl.BlockSpec((1, d), lambda i: (0, 0)),
            pl.BlockSpec((1, 1, d), bvec),
            pl.BlockSpec((1, 1, d), bvec),
            pl.BlockSpec((d, LANES), lambda i: (0, 0)),
            pl.BlockSpec((d, LANES), lambda i: (0, 0)),
            pl.BlockSpec((1, LANES), lambda i: (0, 0)),
        ],
        out_specs=(pl.BlockSpec((tm, d), lambda i: (i, 0)), pl.BlockSpec((tm, LANES), lambda i: (i, 0))),
        compiler_params=_params(("arbitrary",)),
        name="router",
    )(x2, g_ffn.reshape(1, d), shift2.reshape(nb, 1, d), scale2.reshape(nb, 1, d), whi, wlo, bias)


def _final_kernel(x_ref, y0_ref, y1_ref, r_ref, gate_ref, g_ref, o_ref):
    w0 = r_ref[:, 2:3]
    w1 = r_ref[:, 3:4]
    ffn = w0 * y0_ref[...] + w1 * y1_ref[...]
    x = x_ref[...] + gate_ref[0] * ffn
    ms = jnp.mean(x * x, axis=-1, keepdims=True)
    o_ref[...] = (x * lax.rsqrt(ms + RMS_EPS)) * g_ref[...]


def _final(x2, ys, route, gate2, g_final, seq, tm=512):
    t, d = x2.shape
    nb = gate2.shape[0]
    steps_per_seq = seq // tm
    return pl.pallas_call(
        _final_kernel,
        out_shape=jax.ShapeDtypeStruct((t, d), F32),
        grid=(t // tm,),
        in_specs=[
            pl.BlockSpec((tm, d), lambda i: (i, 0)),
            pl.BlockSpec((tm, d), lambda i: (i, 0)),
            pl.BlockSpec((tm, d), lambda i: (i + t // tm, 0)),
            pl.BlockSpec((tm, LANES), lambda i: (i, 0)),
            pl.BlockSpec((1, 1, d), lambda i: (i // steps_per_seq, 0, 0)),
            pl.BlockSpec((1, d), lambda i: (0, 0)),
        ],
        out_specs=pl.BlockSpec((tm, d), lambda i: (i, 0)),
        compiler_params=_params(("arbitrary",)),
        name="final",
    )(x2, ys, ys, route, gate2.reshape(nb, 1, d), g_final.reshape(1, d))


MOE_ROWS = 256
DMA_UNROLL = 8


def _moe_kernel(blk_e, blk_n, row_src, row_dst, h_hbm, wgu_ref, wd_ref, ys_hbm,
                h0, h1, o0, o1, wgu_bf, wd_bf, gsem, ssem):
    b = pl.program_id(0)
    nblocks = pl.num_programs(0)
    n = blk_n[b]
    tm = h0.shape[0]
    dump0 = ys_hbm.shape[0] - 2 * tm
    hbufs, obufs = (h0, h1), (o0, o1)

    def gather_row(blk, j, s):
        return pltpu.make_async_copy(h_hbm.at[pl.ds(row_src[(blk + 1) * tm + j], 1), :],
                                     hbufs[s].at[pl.ds(j, 1), :], gsem.at[s])

    def scatter_row(blk, j, s):
        return pltpu.make_async_copy(obufs[s].at[pl.ds(j, 1), :],
                                     ys_hbm.at[pl.ds(row_dst[(blk + 1) * tm + j], 1), :], ssem.at[s])

    def loop_rows(make, blk, s):
        def body(j, carry):
            make(blk, j, s).start()
            return carry
        lax.fori_loop(0, tm, body, 0, unroll=DMA_UNROLL)

    def wait_gather(s):
        pltpu.make_async_copy(h_hbm.at[pl.ds(0, tm), :], hbufs[s], gsem.at[s]).wait()

    def wait_scatter(s):
        pltpu.make_async_copy(obufs[s], ys_hbm.at[pl.ds(0, tm), :], ssem.at[s]).wait()

    @pl.when(b == 0)
    def _():
        for s in range(2):
            obufs[s][...] = jnp.zeros(obufs[s].shape, obufs[s].dtype)
            dump = pltpu.make_async_copy(obufs[s], ys_hbm.at[pl.ds(dump0 + s * tm, tm), :], ssem.at[s])
            dump.start()
            dump.wait()
        loop_rows(gather_row, 0, 0)

    @pl.when(n > 0)
    def _():
        prev_e = blk_e[jnp.maximum(b - 1, 0)]

        @pl.when(jnp.logical_or(b == 0, blk_e[b] != prev_e))
        def _():
            wgu_bf[...] = wgu_ref[...].astype(BF16)
            wd_bf[...] = wd_ref[...].astype(BF16)

        def half_step(s):
            t = 1 - s
            wait_gather(s)

            @pl.when(b >= 1)
            def _():
                wait_scatter(s)

            for j in range(tm):
                gather_row(b + 1, j, t).start()
                scatter_row(b - 1, j, t).start()
            h = hbufs[s][...].astype(BF16)
            gu = jnp.dot(h, wgu_bf[...], preferred_element_type=F32)
            ff = gu.shape[1] // 2
            gate, up = gu[:, :ff], gu[:, ff:]
            act = (gate * jax.nn.sigmoid(gate) * up).astype(BF16)
            obufs[s][...] = jnp.dot(act, wd_bf[...], preferred_element_type=F32)

            nxt = jnp.minimum(b + 1, nblocks - 1)
            is_last = jnp.logical_or(b + 1 >= nblocks, blk_n[nxt] == 0)

            @pl.when(is_last)
            def _():
                loop_rows(scatter_row, b, s)
                wait_scatter(t)
                wait_scatter(s)
                wait_gather(t)

        for s in range(2):
            pl.when(b % 2 == s)(functools.partial(half_step, s))


def _moe_plan(route, tm, t):
    ids = route[:, 0:TOP_K].astype(jnp.int32)
    flat_e = ids.reshape(-1)
    n_assign = flat_e.shape[0]
    order = jnp.argsort(flat_e).astype(jnp.int32)
    counts = jnp.sum(flat_e[:, None] == jnp.arange(N_EXPERTS, dtype=jnp.int32)[None, :], axis=0,
                     dtype=jnp.int32)
    raw_start = jnp.cumsum(counts) - counts
    nblk = (counts + tm - 1) // tm
    blk_end = jnp.cumsum(nblk)
    n_blocks = -(-(n_assign + N_EXPERTS * (tm - 1)) // tm)
    bidx = jnp.arange(n_blocks, dtype=jnp.int32)
    total = blk_end[-1]
    e_of = jnp.searchsorted(blk_end, jnp.minimum(bidx, total - 1), side='right').astype(jnp.int32)
    e_of = jnp.minimum(e_of, N_EXPERTS - 1)
    j_in = bidx - (blk_end[e_of] - nblk[e_of])
    blk_i0 = raw_start[e_of] + j_in * tm
    blk_n = jnp.where(bidx < total, jnp.clip(counts[e_of] - j_in * tm, 0, tm), 0).astype(jnp.int32)
    j = jnp.arange(tm, dtype=jnp.int32)[None, :]
    last = blk_i0 + jnp.maximum(blk_n, 1) - 1
    idx = jnp.clip(jnp.minimum(blk_i0[:, None] + j, last[:, None]), 0, n_assign - 1)
    a = order[idx]
    tok, k = a // TOP_K, a % TOP_K
    row_src = tok
    row_dst = jnp.where(j < blk_n[:, None], k * t + tok, n_assign + (bidx[:, None] % 2) * tm + j)
    pad_src = jnp.zeros((1, tm), jnp.int32)
    pad_dst = n_assign + tm + j
    row_src = jnp.concatenate([pad_src, row_src, pad_src], axis=0)
    row_dst = jnp.concatenate([pad_dst, row_dst, pad_dst], axis=0)
    return e_of, blk_n, row_src.reshape(-1), row_dst.reshape(-1).astype(jnp.int32), n_blocks


def _moe(h2, route, w_gate_up, w_down, tm=MOE_ROWS):
    t, d = h2.shape
    ff2 = w_gate_up.shape[2]
    n_assign = t * TOP_K
    blk_e, blk_n, row_src, row_dst, n_blocks = _moe_plan(route, tm, t)
    return pl.pallas_call(
        _moe_kernel,
        out_shape=jax.ShapeDtypeStruct((n_assign + 2 * tm, d), F32),
        grid_spec=pltpu.PrefetchScalarGridSpec(
            num_scalar_prefetch=4,
            grid=(n_blocks,),
            in_specs=[
                pl.BlockSpec(memory_space=pl.ANY),
                pl.BlockSpec((None, d, ff2), lambda b, be, bn, rs, rd: (be[b], 0, 0)),
                pl.BlockSpec((None, ff2 // 2, d), lambda b, be, bn, rs, rd: (be[b], 0, 0)),
            ],
            out_specs=pl.BlockSpec(memory_space=pl.ANY),
            scratch_shapes=[pltpu.VMEM((tm, d), F32), pltpu.VMEM((tm, d), F32),
                            pltpu.VMEM((tm, d), F32), pltpu.VMEM((tm, d), F32),
                            pltpu.VMEM((d, ff2), BF16), pltpu.VMEM((ff2 // 2, d), BF16),
                            pltpu.SemaphoreType.DMA((2,)), pltpu.SemaphoreType.DMA((2,))],
        ),
        compiler_params=_params(("arbitrary",)),
        name="moe",
    )(blk_e, blk_n, row_src, row_dst, h2, w_gate_up, w_down)


def kernel(x, c, positions, w_ada, b_ada, g_mix, w_in, b_in, attn_sinks, conv_w, conv_b, conv_ln_g,
           conv_ln_b, w_out, b_out, g_ffn, w_group_router, b_group_router, w_expert_router,
           b_expert_router, w_gate_up, w_down, g_final):
    nb, seq, d = x.shape
    t = nb * seq
    mod = _ada(c, w_ada, b_ada)
    shift1, scale1, gate1, shift2, scale2, gate2 = [mod[:, i * d:(i + 1) * d] for i in range(6)]
    x2d = x.reshape(t, d)
    q, kv, ab = _inproj(x2d, positions.reshape(t, 1), g_mix, shift1, scale1, w_in, b_in, seq)
    attn = _attn(q, kv, attn_sinks, nb, seq)
    x2 = _mix(ab, attn, x2d, conv_w, conv_b, conv_ln_g, conv_ln_b, w_out, b_out, gate1, nb, seq)
    h2, route = _router(x2, g_ffn, shift2, scale2, w_group_router, b_group_router,
                        w_expert_router, b_expert_router, seq)
    ys = _moe(h2, route, w_gate_up, w_down)
    out = _final(x2, ys, route, gate2, g_final, seq)
    return out.reshape(nb, seq, d)
```

```python
import functools
import math

import jax
import jax.numpy as jnp
from jax import lax
from jax.experimental import pallas as pl
from jax.experimental.pallas import tpu as pltpu

F32 = jnp.float32
BF16 = jnp.bfloat16

HEAD_DIM = 64
N_Q_HEADS = 16
N_KV_HEADS = 2
ATTN_WIDTH = N_Q_HEADS * HEAD_DIM
KV_WIDTH = N_KV_HEADS * HEAD_DIM
CONV_SIZE = 31
WINDOW = 128
ROPE_THETA = 10000.0
N_GROUPS = 8
EXPERTS_PER_GROUP = 8
N_EXPERTS = N_GROUPS * EXPERTS_PER_GROUP
TOP_K = 2
EXPERT_FF = 512
RMS_EPS = 1e-6
LN_EPS = 1e-5
NEG_INF = -1e30

LANES = 128
SUBLANES = 8
VMEM_LIMIT = 56 * 1024 * 1024


def _params(semantics, vmem=VMEM_LIMIT):
    return pltpu.CompilerParams(dimension_semantics=semantics, vmem_limit_bytes=vmem)


def _ada_kernel(cb_ref, w_ref, b_ref, o_ref):
    nb, d = cb_ref.shape[0], w_ref.shape[0]
    nl = w_ref.shape[1] // LANES

    def body(k, acc):
        r = pl.multiple_of(k * SUBLANES, SUBLANES)
        w8 = w_ref[pl.ds(r, SUBLANES), :]
        new = []
        for b in range(nb):
            cv = cb_ref[b, pl.ds(r, SUBLANES), :]
            sv = cv * jax.nn.sigmoid(cv)
            for j in range(nl):
                new.append(acc[b * nl + j] + w8[:, j * LANES:(j + 1) * LANES] * sv)
        return tuple(new)

    init = tuple(jnp.zeros((SUBLANES, LANES), F32) for _ in range(nb * nl))
    acc = lax.fori_loop(0, d // SUBLANES, body, init)
    for b in range(nb):
        row = jnp.concatenate(
            [jnp.sum(acc[b * nl + j], axis=0, keepdims=True) for j in range(nl)], axis=1)
        o_ref[b:b + 1, :] = row + b_ref[...]


def _ada(c, w_ada, b_ada, tn=1024):
    nb, d = c.shape
    n = w_ada.shape[1]
    cb = jnp.broadcast_to(c[:, :, None], (nb, d, LANES))
    return pl.pallas_call(
        _ada_kernel,
        out_shape=jax.ShapeDtypeStruct((nb, n), F32),
        grid=(n // tn,),
        in_specs=[
            pl.BlockSpec((nb, d, LANES), lambda j: (0, 0, 0)),
            pl.BlockSpec((d, tn), lambda j: (0, j)),
            pl.BlockSpec((1, tn), lambda j: (0, j)),
        ],
        out_specs=pl.BlockSpec((nb, tn), lambda j: (0, j)),
        compiler_params=_params(("arbitrary",)),
        name="ada",
    )(cb, w_ada, b_ada.reshape(1, n))


def _rmsnorm_mod(x, g, shift, scale):
    ms = jnp.mean(x * x, axis=-1, keepdims=True)
    h = (x * lax.rsqrt(ms + RMS_EPS)) * g
    return h * (1.0 + scale) + shift


def _inproj_kernel(x_ref, pos_ref, g_ref, sh_ref, sc_ref, w_ref, b_ref, invf_ref, sgn_ref,
                   q_ref, kv_ref, ab_ref):
    h = _rmsnorm_mod(x_ref[...], g_ref[...], sh_ref[0], sc_ref[0]).astype(BF16)

    ang = pos_ref[...].astype(F32) * invf_ref[...]
    cosv = jnp.cos(ang)
    sinv = jnp.sin(ang) * sgn_ref[...]
    lane = lax.broadcasted_iota(jnp.int32, (1, LANES), 1)
    first_half = (lane % HEAD_DIM) < (HEAD_DIM // 2)

    def rope(t):
        rot = jnp.where(first_half, pltpu.roll(t, LANES - HEAD_DIM // 2, 1),
                        pltpu.roll(t, HEAD_DIM // 2, 1))
        return t * cosv + rot * sinv

    def proj(c0, width):
        return (jnp.dot(h, w_ref[:, c0:c0 + width], preferred_element_type=F32)
                + b_ref[:, c0:c0 + width])

    qscale = 1.0 / math.sqrt(HEAD_DIM)
    step = 512
    for c0 in range(0, ATTN_WIDTH, step):
        z = proj(c0, step)
        for j in range(step // LANES):
            q_ref[:, c0 + j * LANES:c0 + (j + 1) * LANES] = (
                rope(z[:, j * LANES:(j + 1) * LANES]) * qscale).astype(BF16)
    z = proj(ATTN_WIDTH, 2 * KV_WIDTH)
    k = rope(z[:, :KV_WIDTH])
    v = z[:, KV_WIDTH:]
    kv_ref[:, 0 * LANES:1 * LANES] = k.astype(BF16)
    kv_ref[:, 1 * LANES:2 * LANES] = pltpu.roll(k, HEAD_DIM, 1).astype(BF16)
    kv_ref[:, 2 * LANES:3 * LANES] = v.astype(BF16)
    kv_ref[:, 3 * LANES:4 * LANES] = pltpu.roll(v, HEAD_DIM, 1).astype(BF16)
    base = ATTN_WIDTH + 2 * KV_WIDTH
    for c0 in range(0, ab_ref.shape[1], step):
        ab_ref[:, c0:c0 + step] = proj(base + c0, step).astype(BF16)


def _inproj(x2d, pos2d, g_mix, shift1, scale1, w_in, b_in, seq, tm=512):
    t, d = x2d.shape
    n = w_in.shape[1]
    nb = shift1.shape[0]
    half = HEAD_DIM // 2
    inv_freq = ROPE_THETA ** (-jnp.arange(half, dtype=F32) * 2.0 / HEAD_DIM)
    invf = jnp.tile(inv_freq, LANES // half).reshape(1, LANES)
    sgn = jnp.tile(jnp.concatenate([-jnp.ones((half,), F32), jnp.ones((half,), F32)]),
                   LANES // HEAD_DIM).reshape(1, LANES)
    steps_per_seq = seq // tm
    conv_w2 = n - ATTN_WIDTH - 2 * KV_WIDTH
    bvec = lambda i: (i // steps_per_seq, 0, 0)
    return pl.pallas_call(
        _inproj_kernel,
        out_shape=(jax.ShapeDtypeStruct((t, ATTN_WIDTH), BF16),
                   jax.ShapeDtypeStruct((t, 4 * KV_WIDTH), BF16),
                   jax.ShapeDtypeStruct((t, conv_w2), BF16)),
        grid=(t // tm,),
        in_specs=[
            pl.BlockSpec((tm, d), lambda i: (i, 0)),
            pl.BlockSpec((tm, 1), lambda i: (i, 0)),
            pl.BlockSpec((1, d), lambda i: (0, 0)),
            pl.BlockSpec((1, 1, d), bvec),
            pl.BlockSpec((1, 1, d), bvec),
            pl.BlockSpec((d, n), lambda i: (0, 0)),
            pl.BlockSpec((1, n), lambda i: (0, 0)),
            pl.BlockSpec((1, LANES), lambda i: (0, 0)),
            pl.BlockSpec((1, LANES), lambda i: (0, 0)),
        ],
        out_specs=(pl.BlockSpec((tm, ATTN_WIDTH), lambda i: (i, 0)),
                   pl.BlockSpec((tm, 4 * KV_WIDTH), lambda i: (i, 0)),
                   pl.BlockSpec((tm, conv_w2), lambda i: (i, 0))),
        compiler_params=_params(("arbitrary",)),
        name="inproj",
    )(x2d, pos2d, g_mix.reshape(1, d), shift1.reshape(nb, 1, d), scale1.reshape(nb, 1, d),
      w_in.astype(BF16), b_in.reshape(1, n), invf, sgn)


def _attn_kernel(sink_ref, q_ref, kvm_ref, kvh_ref, o_ref):
    i = pl.program_id(1)
    tq = q_ref.shape[0]
    blk = WINDOW
    heads_per_group = N_Q_HEADS // N_KV_HEADS
    pairs = heads_per_group // 2
    row = lax.broadcasted_iota(jnp.int32, (blk, 2 * blk), 0)
    col = lax.broadcasted_iota(jnp.int32, (blk, 2 * blk), 1)
    rel = row + blk - col
    band = (rel >= 0) & (rel < WINDOW)
    lane = lax.broadcasted_iota(jnp.int32, (1, LANES), 1)
    lo = lane < HEAD_DIM
    zero = jnp.zeros((), BF16)

    for j in range(tq // blk):
        if j == 0:
            kv = jnp.concatenate([kvh_ref[...], kvm_ref[0:blk, :]], axis=0)
            valid = band & ((col >= blk) | (i > 0))
        else:
            kv = kvm_ref[(j - 1) * blk:(j + 1) * blk, :]
            valid = band
        k_nat, k_swp = kv[:, 0:LANES], kv[:, LANES:2 * LANES]
        v_nat, v_swp = kv[:, 2 * LANES:3 * LANES], kv[:, 3 * LANES:4 * LANES]
        for g in range(N_KV_HEADS):
            src_lo_k, src_hi_k = (k_nat, k_swp) if g == 0 else (k_swp, k_nat)
            src_lo_v, src_hi_v = (v_nat, v_swp) if g == 0 else (v_swp, v_nat)
            k_even = jnp.where(lo, src_lo_k, zero)
            k_odd = jnp.where(lo, zero, src_hi_k)
            v_even = jnp.where(lo, src_lo_v, zero)
            v_odd = jnp.where(lo, zero, src_hi_v)
            qs = jnp.concatenate(
                [q_ref[j * blk:(j + 1) * blk, (g * pairs + p) * LANES:(g * pairs + p + 1) * LANES]
                 for p in range(pairs)], axis=0)
            nt = (((1,), (1,)), ((), ()))
            s_even = lax.dot_general(qs, k_even, nt, preferred_element_type=F32)
            s_odd = lax.dot_general(qs, k_odd, nt, preferred_element_type=F32)
            probs, rinv = [], []
            for s_all, parity in ((s_even, 0), (s_odd, 1)):
                ps, rs = [], []
                for p in range(pairs):
                    sink = sink_ref[g * heads_per_group + 2 * p + parity]
                    s = jnp.where(valid, s_all[p * blk:(p + 1) * blk, :], NEG_INF)
                    m = jnp.maximum(jnp.max(s, axis=-1, keepdims=True), sink)
                    e = jnp.exp(s - m)
                    den = jnp.sum(e, axis=-1, keepdims=True) + jnp.exp(sink - m)
                    ps.append(e.astype(BF16))
                    rs.append(1.0 / den)
                probs.append(jnp.concatenate(ps, axis=0))
                rinv.append(rs)
            o = (jnp.dot(probs[0], v_even, preferred_element_type=F32)
                 + jnp.dot(probs[1], v_odd, preferred_element_type=F32))
            for p in range(pairs):
                scale = jnp.where(lo, rinv[0][p], rinv[1][p])
                c = (g * pairs + p) * LANES
                o_ref[j * blk:(j + 1) * blk, c:c + LANES] = (
                    o[p * blk:(p + 1) * blk, :] * scale).astype(BF16)


def _attn(q, kv, sinks, nb, seq, tq=512):
    blk = WINDOW
    q3 = q.reshape(nb, seq, ATTN_WIDTH)
    kv3 = kv.reshape(nb, seq, 4 * KV_WIDTH)
    sub = tq // blk
    out = pl.pallas_call(
        _attn_kernel,
        out_shape=jax.ShapeDtypeStruct((nb, seq, ATTN_WIDTH), BF16),
        grid=(nb, seq // tq),
        in_specs=[
            pl.BlockSpec(memory_space=pltpu.SMEM),
            pl.BlockSpec((None, tq, ATTN_WIDTH), lambda b, i: (b, i, 0)),
            pl.BlockSpec((None, tq, 4 * KV_WIDTH), lambda b, i: (b, i, 0)),
            pl.BlockSpec((None, blk, 4 * KV_WIDTH), lambda b, i: (b, jnp.maximum(i * sub - 1, 0), 0)),
        ],
        out_specs=pl.BlockSpec((None, tq, ATTN_WIDTH), lambda b, i: (b, i, 0)),
        compiler_params=_params(("arbitrary", "arbitrary")),
        name="attn",
    )(sinks.astype(F32), q3, kv3, kv3)
    return out.reshape(nb * seq, ATTN_WIDTH)


CONV_HALO = 32


def _mix_kernel(ab_ref, abh_ref, attn_ref, x_ref, wc_ref, cb_ref, lg_ref, lb_ref, wo_ref, bo_ref,
                gate_ref, o_ref, us_ref, y_ref):
    i = pl.program_id(1)
    tt = x_ref.shape[0]
    cw = y_ref.shape[1]

    def glu(ab):
        a = ab[:, :cw].astype(F32)
        b = ab[:, cw:].astype(F32)
        return a * jax.nn.sigmoid(b)

    halo = glu(abh_ref[...])
    us_ref[0, 0:CONV_HALO, :] = jnp.where(i > 0, halo, 0.0)
    us_ref[0, CONV_HALO:, :] = glu(ab_ref[...])
    n = tt + CONV_HALO - SUBLANES
    for s in range(1, SUBLANES):
        us_ref[s, 0:n, :] = us_ref[0, s:s + n, :]

    off = CONV_HALO - (CONV_SIZE - 1)

    def conv_lanes(c, carry):
        l0 = pl.multiple_of(c * LANES, LANES)
        acc = jnp.zeros((tt, LANES), F32)
        for k in range(CONV_SIZE):
            o = off + k
            a0 = (o // SUBLANES) * SUBLANES
            acc = acc + (us_ref[o % SUBLANES, a0:a0 + tt, pl.ds(l0, LANES)]
                         * wc_ref[k:k + 1, pl.ds(l0, LANES)])
        y_ref[:, pl.ds(l0, LANES)] = acc
        return carry

    lax.fori_loop(0, cw // LANES, conv_lanes, 0)

    y = y_ref[...] + cb_ref[...]
    mu = jnp.mean(y, axis=-1, keepdims=True)
    yc = y - mu
    var = jnp.mean(yc * yc, axis=-1, keepdims=True)
    yn = yc * lax.rsqrt(var + LN_EPS) * lg_ref[...] + lb_ref[...]
    conv = (yn * jax.nn.sigmoid(yn)).astype(BF16)

    aw = attn_ref.shape[1]
    mixed = (jnp.dot(attn_ref[...], wo_ref[0:aw, :], preferred_element_type=F32)
             + jnp.dot(conv, wo_ref[aw:, :], preferred_element_type=F32) + bo_ref[...])
    o_ref[...] = x_ref[...] + gate_ref[0] * mixed


def _mix(ab, attn, x2d, conv_w, conv_b, ln_g, ln_b, w_out, b_out, gate1, nb, seq, tt=256):
    t, d = x2d.shape
    cw = conv_w.shape[1]
    aw = attn.shape[1]
    ab3 = ab.reshape(nb, seq, 2 * cw)
    attn3 = attn.reshape(nb, seq, aw)
    x3 = x2d.reshape(nb, seq, d)
    hsub = tt // CONV_HALO
    out = pl.pallas_call(
        _mix_kernel,
        out_shape=jax.ShapeDtypeStruct((nb, seq, d), F32),
        grid=(nb, seq // tt),
        in_specs=[
            pl.BlockSpec((None, tt, 2 * cw), lambda b, i: (b, i, 0)),
            pl.BlockSpec((None, CONV_HALO, 2 * cw), lambda b, i: (b, jnp.maximum(i * hsub - 1, 0), 0)),
            pl.BlockSpec((None, tt, aw), lambda b, i: (b, i, 0)),
            pl.BlockSpec((None, tt, d), lambda b, i: (b, i, 0)),
            pl.BlockSpec((CONV_SIZE, cw), lambda b, i: (0, 0)),
            pl.BlockSpec((1, cw), lambda b, i: (0, 0)),
            pl.BlockSpec((1, cw), lambda b, i: (0, 0)),
            pl.BlockSpec((1, cw), lambda b, i: (0, 0)),
            pl.BlockSpec((d, d), lambda b, i: (0, 0)),
            pl.BlockSpec((1, d), lambda b, i: (0, 0)),
            pl.BlockSpec((1, 1, d), lambda b, i: (b, 0, 0)),
        ],
        out_specs=pl.BlockSpec((None, tt, d), lambda b, i: (b, i, 0)),
        scratch_shapes=[pltpu.VMEM((SUBLANES, CONV_HALO + tt, cw), F32), pltpu.VMEM((tt, cw), F32)],
        compiler_params=_params(("arbitrary", "arbitrary")),
        name="mix",
    )(ab3, ab3, attn3, x3, conv_w, conv_b.reshape(1, cw), ln_g.reshape(1, cw), ln_b.reshape(1, cw),
      w_out.astype(BF16), b_out.reshape(1, d), gate1.reshape(nb, 1, d))
    return out.reshape(t, d)


def _split_bf16(a):
    hi = a.astype(BF16)
    lo = (a - hi.astype(F32)).astype(BF16)
    return hi, lo


def _router_kernel(x_ref, g_ref, sh_ref, sc_ref, whi_ref, wlo_ref, b_ref, h_ref, r_ref):
    h = _rmsnorm_mod(x_ref[...], g_ref[...], sh_ref[0], sc_ref[0])
    h_ref[...] = h
    hi, lo = _split_bf16(h)
    logits = (jnp.dot(hi, whi_ref[...], preferred_element_type=F32)
              + jnp.dot(hi, wlo_ref[...], preferred_element_type=F32)
              + jnp.dot(lo, whi_ref[...], preferred_element_type=F32)) + b_ref[...]
    tm = logits.shape[0]
    lane = lax.broadcasted_iota(jnp.int32, (tm, LANES), 1)
    big = jnp.int32(LANES)

    def first_argmax(v):
        m = jnp.max(v, axis=-1, keepdims=True)
        idx = jnp.min(jnp.where(v == m, lane, big), axis=-1, keepdims=True)
        return m, idx

    is_group = lane < N_GROUPS
    gl = jnp.where(is_group, logits, NEG_INF)
    gmax, gsel = first_argmax(gl)
    p_g = 1.0 / jnp.sum(jnp.where(is_group, jnp.exp(gl - gmax), 0.0), axis=-1, keepdims=True)
    e_lo = N_GROUPS + gsel * EXPERTS_PER_GROUP
    in_group = (lane >= e_lo) & (lane < e_lo + EXPERTS_PER_GROUP)
    el = jnp.where(in_group, logits, NEG_INF)
    t1, i1 = first_argmax(el)
    el2 = jnp.where(lane == i1, NEG_INF, el)
    t2, i2 = first_argmax(el2)
    e2 = jnp.exp(t2 - t1)
    w1 = p_g / (1.0 + e2)
    w2 = p_g * e2 / (1.0 + e2)
    id1 = (i1 - N_GROUPS).astype(F32)
    id2 = (i2 - N_GROUPS).astype(F32)
    r_ref[...] = jnp.where(lane == 0, id1, jnp.where(lane == 1, id2,
                           jnp.where(lane == 2, w1, jnp.where(lane == 3, w2, 0.0))))


def _router(x2, g_ffn, shift2, scale2, w_gr, b_gr, w_er, b_er, seq, tm=512):
    t, d = x2.shape
    nb = shift2.shape[0]
    w = jnp.zeros((d, LANES), F32).at[:, :N_GROUPS].set(w_gr).at[:, N_GROUPS:N_GROUPS + N_EXPERTS].set(w_er)
    bias = jnp.zeros((1, LANES), F32).at[0, :N_GROUPS].set(b_gr).at[0, N_GROUPS:N_GROUPS + N_EXPERTS].set(b_er)
    whi = w.astype(BF16)
    wlo = (w - whi.astype(F32)).astype(BF16)
    steps_per_seq = seq // tm
    bvec = lambda i: (i // steps_per_seq, 0, 0)
    return pl.pallas_call(
        _router_kernel,
        out_shape=(jax.ShapeDtypeStruct((t, d), F32), jax.ShapeDtypeStruct((t, LANES), F32)),
        grid=(t // tm,),
        in_specs=[
            pl.BlockSpec((tm, d), lambda i: (i, 0)),
            pl.BlockSpec((1, d), lambda i: (0, 0)),
            pl.BlockSpec((1, 1, d), bvec),
            pl.BlockSpec((1, 1, d), bvec),
            pl.BlockSpec((d, LANES), lambda i: (0, 0)),
            pl.BlockSpec((d, LANES), lambda i: (0, 0)),
            pl.BlockSpec((1, LANES), lambda i: (0, 0)),
        ],
        out_specs=(pl.BlockSpec((tm, d), lambda i: (i, 0)), pl.BlockSpec((tm, LANES), lambda i: (i, 0))),
        compiler_params=_params(("arbitrary",)),
        name="router",
    )(x2, g_ffn.reshape(1, d), shift2.reshape(nb, 1, d), scale2.reshape(nb, 1, d), whi, wlo, bias)


def _final_kernel(x_ref, y0_ref, y1_ref, r_ref, gate_ref, g_ref, o_ref):
    w0 = r_ref[:, 2:3]
    w1 = r_ref[:, 3:4]
    ffn = w0 * y0_ref[...] + w1 * y1_ref[...]
    x = x_ref[...] + gate_ref[0] * ffn
    ms = jnp.mean(x * x, axis=-1, keepdims=True)
    o_ref[...] = (x * lax.rsqrt(ms + RMS_EPS)) * g_ref[...]


def _final(x2, ys, route, gate2, g_final, seq, tm=512):
    t, d = x2.shape
    nb = gate2.shape[0]
    steps_per_seq = seq // tm
    return pl.pallas_call(
        _final_kernel,
        out_shape=jax.ShapeDtypeStruct((t, d), F32),
        grid=(t // tm,),
        in_specs=[
            pl.BlockSpec((tm, d), lambda i: (i, 0)),
            pl.BlockSpec((tm, d), lambda i: (i, 0)),
            pl.BlockSpec((tm, d), lambda i: (i + t // tm, 0)),
            pl.BlockSpec((tm, LANES), lambda i: (i, 0)),
            pl.BlockSpec((1, 1, d), lambda i: (i // steps_per_seq, 0, 0)),
            pl.BlockSpec((1, d), lambda i: (0, 0)),
        ],
        out_specs=pl.BlockSpec((tm, d), lambda i: (i, 0)),
        compiler_params=_params(("arbitrary",)),
        name="final",
    )(x2, ys, ys, route, gate2.reshape(nb, 1, d), g_final.reshape(1, d))


MOE_ROWS = 256
DMA_UNROLL = 8


def _moe_kernel(blk_e, blk_n, row_src, row_dst, h_hbm, wgu_ref, wd_ref, ys_hbm,
                h0, h1, o0, o1, wgu_bf, wd_bf, gsem, ssem):
    b = pl.program_id(0)
    nblocks = pl.num_programs(0)
    n = blk_n[b]
    tm = h0.shape[0]
    dump0 = ys_hbm.shape[0] - 2 * tm
    hbufs, obufs = (h0, h1), (o0, o1)

    def gather_row(blk, j, s):
        return pltpu.make_async_copy(h_hbm.at[pl.ds(row_src[(blk + 1) * tm + j], 1), :],
                                     hbufs[s].at[pl.ds(j, 1), :], gsem.at[s])

    def scatter_row(blk, j, s):
        return pltpu.make_async_copy(obufs[s].at[pl.ds(j, 1), :],
                                     ys_hbm.at[pl.ds(row_dst[(blk + 1) * tm + j], 1), :], ssem.at[s])

    def loop_rows(make, blk, s):
        def body(j, carry):
            make(blk, j, s).start()
            return carry
        lax.fori_loop(0, tm, body, 0, unroll=DMA_UNROLL)

    def wait_gather(s):
        pltpu.make_async_copy(h_hbm.at[pl.ds(0, tm), :], hbufs[s], gsem.at[s]).wait()

    def wait_scatter(s):
        pltpu.make_async_copy(obufs[s], ys_hbm.at[pl.ds(0, tm), :], ssem.at[s]).wait()

    @pl.when(b == 0)
    def _():
        for s in range(2):
            obufs[s][...] = jnp.zeros(obufs[s].shape, obufs[s].dtype)
            dump = pltpu.make_async_copy(obufs[s], ys_hbm.at[pl.ds(dump0 + s * tm, tm), :], ssem.at[s])
            dump.start()
            dump.wait()
        loop_rows(gather_row, 0, 0)

    @pl.when(n > 0)
    def _():
        prev_e = blk_e[jnp.maximum(b - 1, 0)]

        @pl.when(jnp.logical_or(b == 0, blk_e[b] != prev_e))
        def _():
            wgu_bf[...] = wgu_ref[...].astype(BF16)
            wd_bf[...] = wd_ref[...].astype(BF16)

        def half_step(s):
            t = 1 - s
            wait_gather(s)

            @pl.when(b >= 1)
            def _():
                wait_scatter(s)

            for j in range(tm):
                gather_row(b + 1, j, t).start(priority=j % 2)
                scatter_row(b - 1, j, t).start(priority=j % 2)
            h = hbufs[s][...].astype(BF16)
            gu = jnp.dot(h, wgu_bf[...], preferred_element_type=F32)
            ff = gu.shape[1] // 2
            gate, up = gu[:, :ff], gu[:, ff:]
            act = (gate * jax.nn.sigmoid(gate) * up).astype(BF16)
            obufs[s][...] = jnp.dot(act, wd_bf[...], preferred_element_type=F32)

            nxt = jnp.minimum(b + 1, nblocks - 1)
            is_last = jnp.logical_or(b + 1 >= nblocks, blk_n[nxt] == 0)

            @pl.when(is_last)
            def _():
                loop_rows(scatter_row, b, s)
                wait_scatter(t)
                wait_scatter(s)
                wait_gather(t)

        for s in range(2):
            pl.when(b % 2 == s)(functools.partial(half_step, s))


def _moe_plan(route, tm, t):
    ids = route[:, 0:TOP_K].astype(jnp.int32)
    flat_e = ids.reshape(-1)
    n_assign = flat_e.shape[0]
    order = jnp.argsort(flat_e).astype(jnp.int32)
    counts = jnp.sum(flat_e[:, None] == jnp.arange(N_EXPERTS, dtype=jnp.int32)[None, :], axis=0,
                     dtype=jnp.int32)
    raw_start = jnp.cumsum(counts) - counts
    nblk = (counts + tm - 1) // tm
    blk_end = jnp.cumsum(nblk)
    n_blocks = -(-(n_assign + N_EXPERTS * (tm - 1)) // tm)
    bidx = jnp.arange(n_blocks, dtype=jnp.int32)
    total = blk_end[-1]
    e_of = jnp.searchsorted(blk_end, jnp.minimum(bidx, total - 1), side='right').astype(jnp.int32)
    e_of = jnp.minimum(e_of, N_EXPERTS - 1)
    j_in = bidx - (blk_end[e_of] - nblk[e_of])
    blk_i0 = raw_start[e_of] + j_in * tm
    blk_n = jnp.where(bidx < total, jnp.clip(counts[e_of] - j_in * tm, 0, tm), 0).astype(jnp.int32)
    j = jnp.arange(tm, dtype=jnp.int32)[None, :]
    last = blk_i0 + jnp.maximum(blk_n, 1) - 1
    idx = jnp.clip(jnp.minimum(blk_i0[:, None] + j, last[:, None]), 0, n_assign - 1)
    a = order[idx]
    tok, k = a // TOP_K, a % TOP_K
    row_src = tok
    row_dst = jnp.where(j < blk_n[:, None], k * t + tok, n_assign + (bidx[:, None] % 2) * tm + j)
    pad_src = jnp.zeros((1, tm), jnp.int32)
    pad_dst = n_assign + tm + j
    row_src = jnp.concatenate([pad_src, row_src, pad_src], axis=0)
    row_dst = jnp.concatenate([pad_dst, row_dst, pad_dst], axis=0)
    return e_of, blk_n, row_src.reshape(-1), row_dst.reshape(-1).astype(jnp.int32), n_blocks


def _moe(h2, route, w_gate_up, w_down, tm=MOE_ROWS):
    t, d = h2.shape
    ff2 = w_gate_up.shape[2]
    n_assign = t * TOP_K
    blk_e, blk_n, row_src, row_dst, n_blocks = _moe_plan(route, tm, t)
    return pl.pallas_call(
        _moe_kernel,
        out_shape=jax.ShapeDtypeStruct((n_assign + 2 * tm, d), F32),
        grid_spec=pltpu.PrefetchScalarGridSpec(
            num_scalar_prefetch=4,
            grid=(n_blocks,),
            in_specs=[
                pl.BlockSpec(memory_space=pl.ANY),
                pl.BlockSpec((None, d, ff2), lambda b, be, bn, rs, rd: (be[b], 0, 0)),
                pl.BlockSpec((None, ff2 // 2, d), lambda b, be, bn, rs, rd: (be[b], 0, 0)),
            ],
            out_specs=pl.BlockSpec(memory_space=pl.ANY),
            scratch_shapes=[pltpu.VMEM((tm, d), F32), pltpu.VMEM((tm, d), F32),
                            pltpu.VMEM((tm, d), F32), pltpu.VMEM((tm, d), F32),
                            pltpu.VMEM((d, ff2), BF16), pltpu.VMEM((ff2 // 2, d), BF16),
                            pltpu.SemaphoreType.DMA((2,)), pltpu.SemaphoreType.DMA((2,))],
        ),
        compiler_params=_params(("arbitrary",)),
        name="moe",
    )(blk_e, blk_n, row_src, row_dst, h2, w_gate_up, w_down)


def kernel(x, c, positions, w_ada, b_ada, g_mix, w_in, b_in, attn_sinks, conv_w, conv_b, conv_ln_g,
           conv_ln_b, w_out, b_out, g_ffn, w_group_router, b_group_router, w_expert_router,
           b_expert_router, w_gate_up, w_down, g_final):
    nb, seq, d = x.shape
    t = nb * seq
    mod = _ada(c, w_ada, b_ada)
    shift1, scale1, gate1, shift2, scale2, gate2 = [mod[:, i * d:(i + 1) * d] for i in range(6)]
    x2d = x.reshape(t, d)
    q, kv, ab = _inproj(x2d, positions.reshape(t, 1), g_mix, shift1, scale1, w_in, b_in, seq)
    attn = _attn(q, kv, attn_sinks, nb, seq)
    x2 = _mix(ab, attn, x2d, conv_w, conv_b, conv_ln_g, conv_ln_b, w_out, b_out, gate1, nb, seq)
    h2, route = _router(x2, g_ffn, shift2, scale2, w_group_router, b_group_router,
                        w_expert_router, b_expert_router, seq)
    ys = _moe(h2, route, w_gate_up, w_down)
    out = _final(x2, ys, route, gate2, g_final, seq)
    return out.reshape(nb, seq, d)
```

```python
import functools
import math

import jax
import jax.numpy as jnp
from jax import lax
from jax.experimental import pallas as pl
from jax.experimental.pallas import tpu as pltpu

F32 = jnp.float32
BF16 = jnp.bfloat16

HEAD_DIM = 64
N_Q_HEADS = 16
N_KV_HEADS = 2
ATTN_WIDTH = N_Q_HEADS * HEAD_DIM
KV_WIDTH = N_KV_HEADS * HEAD_DIM
CONV_SIZE = 31
WINDOW = 128
ROPE_THETA = 10000.0
N_GROUPS = 8
EXPERTS_PER_GROUP = 8
N_EXPERTS = N_GROUPS * EXPERTS_PER_GROUP
TOP_K = 2
EXPERT_FF = 512
RMS_EPS = 1e-6
LN_EPS = 1e-5
NEG_INF = -1e30

LANES = 128
SUBLANES = 8
VMEM_LIMIT = 56 * 1024 * 1024


def _params(semantics, vmem=VMEM_LIMIT):
    return pltpu.CompilerParams(dimension_semantics=semantics, vmem_limit_bytes=vmem)


def _ada_kernel(cb_ref, w_ref, b_ref, o_ref):
    nb, d = cb_ref.shape[0], w_ref.shape[0]
    nl = w_ref.shape[1] // LANES

    def body(k, acc):
        r = pl.multiple_of(k * SUBLANES, SUBLANES)
        w8 = w_ref[pl.ds(r, SUBLANES), :]
        new = []
        for b in range(nb):
            cv = cb_ref[b, pl.ds(r, SUBLANES), :]
            sv = cv * jax.nn.sigmoid(cv)
            for j in range(nl):
                new.append(acc[b * nl + j] + w8[:, j * LANES:(j + 1) * LANES] * sv)
        return tuple(new)

    init = tuple(jnp.zeros((SUBLANES, LANES), F32) for _ in range(nb * nl))
    acc = lax.fori_loop(0, d // SUBLANES, body, init)
    for b in range(nb):
        row = jnp.concatenate(
            [jnp.sum(acc[b * nl + j], axis=0, keepdims=True) for j in range(nl)], axis=1)
        o_ref[b:b + 1, :] = row + b_ref[...]


def _ada(c, w_ada, b_ada, tn=1024):
    nb, d = c.shape
    n = w_ada.shape[1]
    cb = jnp.broadcast_to(c[:, :, None], (nb, d, LANES))
    return pl.pallas_call(
        _ada_kernel,
        out_shape=jax.ShapeDtypeStruct((nb, n), F32),
        grid=(n // tn,),
        in_specs=[
            pl.BlockSpec((nb, d, LANES), lambda j: (0, 0, 0)),
            pl.BlockSpec((d, tn), lambda j: (0, j)),
            pl.BlockSpec((1, tn), lambda j: (0, j)),
        ],
        out_specs=pl.BlockSpec((nb, tn), lambda j: (0, j)),
        compiler_params=_params(("arbitrary",)),
        name="ada",
    )(cb, w_ada, b_ada.reshape(1, n))


def _rmsnorm_mod(x, g, shift, scale):
    ms = jnp.mean(x * x, axis=-1, keepdims=True)
    h = (x * lax.rsqrt(ms + RMS_EPS)) * g
    return h * (1.0 + scale) + shift


def _inproj_kernel(x_ref, pos_ref, g_ref, sh_ref, sc_ref, w_ref, b_ref, invf_ref, sgn_ref,
                   q_ref, kv_ref, ab_ref):
    h = _rmsnorm_mod(x_ref[...], g_ref[...], sh_ref[0], sc_ref[0]).astype(BF16)

    ang = pos_ref[...].astype(F32) * invf_ref[...]
    cosv = jnp.cos(ang)
    sinv = jnp.sin(ang) * sgn_ref[...]
    lane = lax.broadcasted_iota(jnp.int32, (1, LANES), 1)
    first_half = (lane % HEAD_DIM) < (HEAD_DIM // 2)

    def rope(t):
        rot = jnp.where(first_half, pltpu.roll(t, LANES - HEAD_DIM // 2, 1),
                        pltpu.roll(t, HEAD_DIM // 2, 1))
        return t * cosv + rot * sinv

    def proj(c0, width):
        return (jnp.dot(h, w_ref[:, c0:c0 + width], preferred_element_type=F32)
                + b_ref[:, c0:c0 + width])

    qscale = 1.0 / math.sqrt(HEAD_DIM)
    step = 512
    for c0 in range(0, ATTN_WIDTH, step):
        z = proj(c0, step)
        for j in range(step // LANES):
            q_ref[:, c0 + j * LANES:c0 + (j + 1) * LANES] = (
                rope(z[:, j * LANES:(j + 1) * LANES]) * qscale).astype(BF16)
    z = proj(ATTN_WIDTH, 2 * KV_WIDTH)
    k = rope(z[:, :KV_WIDTH])
    v = z[:, KV_WIDTH:]
    kv_ref[:, 0 * LANES:1 * LANES] = k.astype(BF16)
    kv_ref[:, 1 * LANES:2 * LANES] = pltpu.roll(k, HEAD_DIM, 1).astype(BF16)
    kv_ref[:, 2 * LANES:3 * LANES] = v.astype(BF16)
    kv_ref[:, 3 * LANES:4 * LANES] = pltpu.roll(v, HEAD_DIM, 1).astype(BF16)
    base = ATTN_WIDTH + 2 * KV_WIDTH
    for c0 in range(0, ab_ref.shape[1], step):
        ab_ref[:, c0:c0 + step] = proj(base + c0, step).astype(BF16)


def _inproj(x2d, pos2d, g_mix, shift1, scale1, w_in, b_in, seq, tm=512):
    t, d = x2d.shape
    n = w_in.shape[1]
    nb = shift1.shape[0]
    half = HEAD_DIM // 2
    inv_freq = ROPE_THETA ** (-jnp.arange(half, dtype=F32) * 2.0 / HEAD_DIM)
    invf = jnp.tile(inv_freq, LANES // half).reshape(1, LANES)
    sgn = jnp.tile(jnp.concatenate([-jnp.ones((half,), F32), jnp.ones((half,), F32)]),
                   LANES // HEAD_DIM).reshape(1, LANES)
    steps_per_seq = seq // tm
    conv_w2 = n - ATTN_WIDTH - 2 * KV_WIDTH
    bvec = lambda i: (i // steps_per_seq, 0, 0)
    return pl.pallas_call(
        _inproj_kernel,
        out_shape=(jax.ShapeDtypeStruct((t, ATTN_WIDTH), BF16),
                   jax.ShapeDtypeStruct((t, 4 * KV_WIDTH), BF16),
                   jax.ShapeDtypeStruct((t, conv_w2), BF16)),
        grid=(t // tm,),
        in_specs=[
            pl.BlockSpec((tm, d), lambda i: (i, 0)),
            pl.BlockSpec((tm, 1), lambda i: (i, 0)),
            pl.BlockSpec((1, d), lambda i: (0, 0)),
            pl.BlockSpec((1, 1, d), bvec),
            pl.BlockSpec((1, 1, d), bvec),
            pl.BlockSpec((d, n), lambda i: (0, 0)),
            pl.BlockSpec((1, n), lambda i: (0, 0)),
            pl.BlockSpec((1, LANES), lambda i: (0, 0)),
            pl.BlockSpec((1, LANES), lambda i: (0, 0)),
        ],
        out_specs=(pl.BlockSpec((tm, ATTN_WIDTH), lambda i: (i, 0)),
                   pl.BlockSpec((tm, 4 * KV_WIDTH), lambda i: (i, 0)),
                   pl.BlockSpec((tm, conv_w2), lambda i: (i, 0))),
        compiler_params=_params(("arbitrary",)),
        name="inproj",
    )(x2d, pos2d, g_mix.reshape(1, d), shift1.reshape(nb, 1, d), scale1.reshape(nb, 1, d),
      w_in.astype(BF16), b_in.reshape(1, n), invf, sgn)


def _attn_kernel(sink_ref, q_ref, kvm_ref, kvh_ref, o_ref):
    i = pl.program_id(1)
    tq = q_ref.shape[0]
    blk = WINDOW
    heads_per_group = N_Q_HEADS // N_KV_HEADS
    pairs = heads_per_group // 2
    row = lax.broadcasted_iota(jnp.int32, (blk, 2 * blk), 0)
    col = lax.broadcasted_iota(jnp.int32, (blk, 2 * blk), 1)
    rel = row + blk - col
    band = (rel >= 0) & (rel < WINDOW)
    lane = lax.broadcasted_iota(jnp.int32, (1, LANES), 1)
    lo = lane < HEAD_DIM
    zero = jnp.zeros((), BF16)

    for j in range(tq // blk):
        if j == 0:
            kv = jnp.concatenate([kvh_ref[...], kvm_ref[0:blk, :]], axis=0)
            valid = band & ((col >= blk) | (i > 0))
        else:
            kv = kvm_ref[(j - 1) * blk:(j + 1) * blk, :]
            valid = band
        k_nat, k_swp = kv[:, 0:LANES], kv[:, LANES:2 * LANES]
        v_nat, v_swp = kv[:, 2 * LANES:3 * LANES], kv[:, 3 * LANES:4 * LANES]
        for g in range(N_KV_HEADS):
            src_lo_k, src_hi_k = (k_nat, k_swp) if g == 0 else (k_swp, k_nat)
            src_lo_v, src_hi_v = (v_nat, v_swp) if g == 0 else (v_swp, v_nat)
            k_even = jnp.where(lo, src_lo_k, zero)
            k_odd = jnp.where(lo, zero, src_hi_k)
            v_even = jnp.where(lo, src_lo_v, zero)
            v_odd = jnp.where(lo, zero, src_hi_v)
            qs = jnp.concatenate(
                [q_ref[j * blk:(j + 1) * blk, (g * pairs + p) * LANES:(g * pairs + p + 1) * LANES]
                 for p in range(pairs)], axis=0)
            nt = (((1,), (1,)), ((), ()))
            s_even = lax.dot_general(qs, k_even, nt, preferred_element_type=F32)
            s_odd = lax.dot_general(qs, k_odd, nt, preferred_element_type=F32)
            probs, rinv = [], []
            for s_all, parity in ((s_even, 0), (s_odd, 1)):
                ps, rs = [], []
                for p in range(pairs):
                    sink = sink_ref[g * heads_per_group + 2 * p + parity]
                    s = jnp.where(valid, s_all[p * blk:(p + 1) * blk, :], NEG_INF)
                    m = jnp.maximum(jnp.max(s, axis=-1, keepdims=True), sink)
                    e = jnp.exp(s - m)
                    den = jnp.sum(e, axis=-1, keepdims=True) + jnp.exp(sink - m)
                    ps.append(e.astype(BF16))
                    rs.append(1.0 / den)
                probs.append(jnp.concatenate(ps, axis=0))
                rinv.append(rs)
            o = (jnp.dot(probs[0], v_even, preferred_element_type=F32)
                 + jnp.dot(probs[1], v_odd, preferred_element_type=F32))
            for p in range(pairs):
                scale = jnp.where(lo, rinv[0][p], rinv[1][p])
                c = (g * pairs + p) * LANES
                o_ref[j * blk:(j + 1) * blk, c:c + LANES] = (
                    o[p * blk:(p + 1) * blk, :] * scale).astype(BF16)


def _attn(q, kv, sinks, nb, seq, tq=512):
    blk = WINDOW
    q3 = q.reshape(nb, seq, ATTN_WIDTH)
    kv3 = kv.reshape(nb, seq, 4 * KV_WIDTH)
    sub = tq // blk
    out = pl.pallas_call(
        _attn_kernel,
        out_shape=jax.ShapeDtypeStruct((nb, seq, ATTN_WIDTH), BF16),
        grid=(nb, seq // tq),
        in_specs=[
            pl.BlockSpec(memory_space=pltpu.SMEM),
            pl.BlockSpec((None, tq, ATTN_WIDTH), lambda b, i: (b, i, 0)),
            pl.BlockSpec((None, tq, 4 * KV_WIDTH), lambda b, i: (b, i, 0)),
            pl.BlockSpec((None, blk, 4 * KV_WIDTH), lambda b, i: (b, jnp.maximum(i * sub - 1, 0), 0)),
        ],
        out_specs=pl.BlockSpec((None, tq, ATTN_WIDTH), lambda b, i: (b, i, 0)),
        compiler_params=_params(("arbitrary", "arbitrary")),
        name="attn",
    )(sinks.astype(F32), q3, kv3, kv3)
    return out.reshape(nb * seq, ATTN_WIDTH)


CONV_HALO = 32


def _mix_kernel(ab_ref, abh_ref, attn_ref, x_ref, wc_ref, cb_ref, lg_ref, lb_ref, wo_ref, bo_ref,
                gate_ref, o_ref, us_ref, y_ref):
    i = pl.program_id(1)
    tt = x_ref.shape[0]
    cw = y_ref.shape[1]

    def glu(ab):
        a = ab[:, :cw].astype(F32)
        b = ab[:, cw:].astype(F32)
        return a * jax.nn.sigmoid(b)

    halo = glu(abh_ref[...])
    us_ref[0, 0:CONV_HALO, :] = jnp.where(i > 0, halo, 0.0)
    us_ref[0, CONV_HALO:, :] = glu(ab_ref[...])
    n = tt + CONV_HALO - SUBLANES
    for s in range(1, SUBLANES):
        us_ref[s, 0:n, :] = us_ref[0, s:s + n, :]

    off = CONV_HALO - (CONV_SIZE - 1)

    def conv_lanes(c, carry):
        l0 = pl.multiple_of(c * LANES, LANES)
        acc = jnp.zeros((tt, LANES), F32)
        for k in range(CONV_SIZE):
            o = off + k
            a0 = (o // SUBLANES) * SUBLANES
            acc = acc + (us_ref[o % SUBLANES, a0:a0 + tt, pl.ds(l0, LANES)]
                         * wc_ref[k:k + 1, pl.ds(l0, LANES)])
        y_ref[:, pl.ds(l0, LANES)] = acc
        return carry

    lax.fori_loop(0, cw // LANES, conv_lanes, 0)

    y = y_ref[...] + cb_ref[...]
    mu = jnp.mean(y, axis=-1, keepdims=True)
    yc = y - mu
    var = jnp.mean(yc * yc, axis=-1, keepdims=True)
    yn = yc * lax.rsqrt(var + LN_EPS) * lg_ref[...] + lb_ref[...]
    conv = (yn * jax.nn.sigmoid(yn)).astype(BF16)

    aw = attn_ref.shape[1]
    mixed = (jnp.dot(attn_ref[...], wo_ref[0:aw, :], preferred_element_type=F32)
             + jnp.dot(conv, wo_ref[aw:, :], preferred_element_type=F32) + bo_ref[...])
    o_ref[...] = x_ref[...] + gate_ref[0] * mixed


def _mix(ab, attn, x2d, conv_w, conv_b, ln_g, ln_b, w_out, b_out, gate1, nb, seq, tt=256):
    t, d = x2d.shape
    cw = conv_w.shape[1]
    aw = attn.shape[1]
    ab3 = ab.reshape(nb, seq, 2 * cw)
    attn3 = attn.reshape(nb, seq, aw)
    x3 = x2d.reshape(nb, seq, d)
    hsub = tt // CONV_HALO
    out = pl.pallas_call(
        _mix_kernel,
        out_shape=jax.ShapeDtypeStruct((nb, seq, d), F32),
        grid=(nb, seq // tt),
        in_specs=[
            pl.BlockSpec((None, tt, 2 * cw), lambda b, i: (b, i, 0)),
            pl.BlockSpec((None, CONV_HALO, 2 * cw), lambda b, i: (b, jnp.maximum(i * hsub - 1, 0), 0)),
            pl.BlockSpec((None, tt, aw), lambda b, i: (b, i, 0)),
            pl.BlockSpec((None, tt, d), lambda b, i: (b, i, 0)),
            pl.BlockSpec((CONV_SIZE, cw), lambda b, i: (0, 0)),
            pl.BlockSpec((1, cw), lambda b, i: (0, 0)),
            pl.BlockSpec((1, cw), lambda b, i: (0, 0)),
            pl.BlockSpec((1, cw), lambda b, i: (0, 0)),
            pl.BlockSpec((d, d), lambda b, i: (0, 0)),
            pl.BlockSpec((1, d), lambda b, i: (0, 0)),
            pl.BlockSpec((1, 1, d), lambda b, i: (b, 0, 0)),
        ],
        out_specs=pl.BlockSpec((None, tt, d), lambda b, i: (b, i, 0)),
        scratch_shapes=[pltpu.VMEM((SUBLANES, CONV_HALO + tt, cw), F32), pltpu.VMEM((tt, cw), F32)],
        compiler_params=_params(("arbitrary", "arbitrary")),
        name="mix",
    )(ab3, ab3, attn3, x3, conv_w, conv_b.reshape(1, cw), ln_g.reshape(1, cw), ln_b.reshape(1, cw),
      w_out.astype(BF16), b_out.reshape(1, d), gate1.reshape(nb, 1, d))
    return out.reshape(t, d)


def _split_bf16(a):
    hi = a.astype(BF16)
    lo = (a - hi.astype(F32)).astype(BF16)
    return hi, lo


PACK_SUB = SUBLANES
HI_MASK = 0xFFFF0000


def _pack_rows(v):
    half = v.shape[1] // 2
    assert half == PACK_SUB * LANES
    bits = lax.bitcast_convert_type(v.astype(BF16).astype(F32), jnp.uint32)
    word = (bits[:, :half] >> 16) | (bits[:, half:] & jnp.uint32(HI_MASK))
    planes = jnp.stack([word[:, s * LANES:(s + 1) * LANES] for s in range(PACK_SUB)], axis=0)
    return pltpu.einshape("smr->msr", planes)


def _unpack_rows(p):
    planes = pltpu.einshape("msr->smr", p)
    lo = [lax.bitcast_convert_type(planes[s] << 16, F32) for s in range(PACK_SUB)]
    hi = [lax.bitcast_convert_type(planes[s] & jnp.uint32(HI_MASK), F32) for s in range(PACK_SUB)]
    return jnp.concatenate(lo + hi, axis=1)


def _router_kernel(x_ref, g_ref, sh_ref, sc_ref, whi_ref, wlo_ref, b_ref, h_ref, r_ref):
    h = _rmsnorm_mod(x_ref[...], g_ref[...], sh_ref[0], sc_ref[0])
    h_ref[...] = _pack_rows(h)
    hi, lo = _split_bf16(h)
    logits = (jnp.dot(hi, whi_ref[...], preferred_element_type=F32)
              + jnp.dot(hi, wlo_ref[...], preferred_element_type=F32)
              + jnp.dot(lo, whi_ref[...], preferred_element_type=F32)) + b_ref[...]
    tm = logits.shape[0]
    lane = lax.broadcasted_iota(jnp.int32, (tm, LANES), 1)
    big = jnp.int32(LANES)

    def first_argmax(v):
        m = jnp.max(v, axis=-1, keepdims=True)
        idx = jnp.min(jnp.where(v == m, lane, big), axis=-1, keepdims=True)
        return m, idx

    is_group = lane < N_GROUPS
    gl = jnp.where(is_group, logits, NEG_INF)
    gmax, gsel = first_argmax(gl)
    p_g = 1.0 / jnp.sum(jnp.where(is_group, jnp.exp(gl - gmax), 0.0), axis=-1, keepdims=True)
    e_lo = N_GROUPS + gsel * EXPERTS_PER_GROUP
    in_group = (lane >= e_lo) & (lane < e_lo + EXPERTS_PER_GROUP)
    el = jnp.where(in_group, logits, NEG_INF)
    t1, i1 = first_argmax(el)
    el2 = jnp.where(lane == i1, NEG_INF, el)
    t2, i2 = first_argmax(el2)
    e2 = jnp.exp(t2 - t1)
    w1 = p_g / (1.0 + e2)
    w2 = p_g * e2 / (1.0 + e2)
    id1 = (i1 - N_GROUPS).astype(F32)
    id2 = (i2 - N_GROUPS).astype(F32)
    r_ref[...] = jnp.where(lane == 0, id1, jnp.where(lane == 1, id2,
                           jnp.where(lane == 2, w1, jnp.where(lane == 3, w2, 0.0))))


def _router(x2, g_ffn, shift2, scale2, w_gr, b_gr, w_er, b_er, seq, tm=512):
    t, d = x2.shape
    nb = shift2.shape[0]
    w = jnp.zeros((d, LANES), F32).at[:, :N_GROUPS].set(w_gr).at[:, N_GROUPS:N_GROUPS + N_EXPERTS].set(w_er)
    bias = jnp.zeros((1, LANES), F32).at[0, :N_GROUPS].set(b_gr).at[0, N_GROUPS:N_GROUPS + N_EXPERTS].set(b_er)
    whi = w.astype(BF16)
    wlo = (w - whi.astype(F32)).astype(BF16)
    steps_per_seq = seq // tm
    bvec = lambda i: (i // steps_per_seq, 0, 0)
    return pl.pallas_call(
        _router_kernel,
        out_shape=(jax.ShapeDtypeStruct((t, PACK_SUB, LANES), jnp.uint32),
                   jax.ShapeDtypeStruct((t, LANES), F32)),
        grid=(t // tm,),
        in_specs=[
            pl.BlockSpec((tm, d), lambda i: (i, 0)),
            pl.BlockSpec((1, d), lambda i: (0, 0)),
            pl.BlockSpec((1, 1, d), bvec),
            pl.BlockSpec((1, 1, d), bvec),
            pl.BlockSpec((d, LANES), lambda i: (0, 0)),
            pl.BlockSpec((d, LANES), lambda i: (0, 0)),
            pl.BlockSpec((1, LANES), lambda i: (0, 0)),
        ],
        out_specs=(pl.BlockSpec((tm, PACK_SUB, LANES), lambda i: (i, 0, 0)),
                   pl.BlockSpec((tm, LANES), lambda i: (i, 0))),
        compiler_params=_params(("arbitrary",)),
        name="router",
    )(x2, g_ffn.reshape(1, d), shift2.reshape(nb, 1, d), scale2.reshape(nb, 1, d), whi, wlo, bias)


def _final_kernel(x_ref, y0_ref, y1_ref, r_ref, gate_ref, g_ref, o_ref):
    w0 = r_ref[:, 2:3]
    w1 = r_ref[:, 3:4]
    ffn = w0 * _unpack_rows(y0_ref[...]) + w1 * _unpack_rows(y1_ref[...])
    x = x_ref[...] + gate_ref[0] * ffn
    ms = jnp.mean(x * x, axis=-1, keepdims=True)
    o_ref[...] = (x * lax.rsqrt(ms + RMS_EPS)) * g_ref[...]


def _final(x2, ys, route, gate2, g_final, seq, tm=512):
    t, d = x2.shape
    nb = gate2.shape[0]
    steps_per_seq = seq // tm
    return pl.pallas_call(
        _final_kernel,
        out_shape=jax.ShapeDtypeStruct((t, d), F32),
        grid=(t // tm,),
        in_specs=[
            pl.BlockSpec((tm, d), lambda i: (i, 0)),
            pl.BlockSpec((tm, PACK_SUB, LANES), lambda i: (i, 0, 0)),
            pl.BlockSpec((tm, PACK_SUB, LANES), lambda i: (i + t // tm, 0, 0)),
            pl.BlockSpec((tm, LANES), lambda i: (i, 0)),
            pl.BlockSpec((1, 1, d), lambda i: (i // steps_per_seq, 0, 0)),
            pl.BlockSpec((1, d), lambda i: (0, 0)),
        ],
        out_specs=pl.BlockSpec((tm, d), lambda i: (i, 0)),
        compiler_params=_params(("arbitrary",)),
        name="final",
    )(x2, ys, ys, route, gate2.reshape(nb, 1, d), g_final.reshape(1, d))


MOE_ROWS = 256
DMA_UNROLL = 8


def _moe_kernel(blk_e, blk_n, row_src, row_dst, h_hbm, wgu_ref, wd_ref, ys_hbm,
                h0, h1, o0, o1, wgu_bf, wd_bf, gsem, ssem):
    b = pl.program_id(0)
    nblocks = pl.num_programs(0)
    n = blk_n[b]
    tm = h0.shape[0]
    dump0 = ys_hbm.shape[0] - 2 * tm
    hbufs, obufs = (h0, h1), (o0, o1)

    def gather_row(blk, j, s):
        return pltpu.make_async_copy(h_hbm.at[row_src[(blk + 1) * tm + j]], hbufs[s].at[j], gsem.at[s])

    def scatter_row(blk, j, s):
        return pltpu.make_async_copy(obufs[s].at[j], ys_hbm.at[row_dst[(blk + 1) * tm + j]], ssem.at[s])

    def loop_rows(make, blk, s):
        def body(j, carry):
            make(blk, j, s).start()
            return carry
        lax.fori_loop(0, tm, body, 0, unroll=DMA_UNROLL)

    def wait_gather(s):
        pltpu.make_async_copy(h_hbm.at[pl.ds(0, tm)], hbufs[s], gsem.at[s]).wait()

    def wait_scatter(s):
        pltpu.make_async_copy(obufs[s], ys_hbm.at[pl.ds(0, tm)], ssem.at[s]).wait()

    @pl.when(b == 0)
    def _():
        for s in range(2):
            obufs[s][...] = jnp.zeros(obufs[s].shape, obufs[s].dtype)
            dump = pltpu.make_async_copy(obufs[s], ys_hbm.at[pl.ds(dump0 + s * tm, tm)], ssem.at[s])
            dump.start()
            dump.wait()
        loop_rows(gather_row, 0, 0)

    @pl.when(n > 0)
    def _():
        prev_e = blk_e[jnp.maximum(b - 1, 0)]

        @pl.when(jnp.logical_or(b == 0, blk_e[b] != prev_e))
        def _():
            wgu_bf[...] = wgu_ref[...].astype(BF16)
            wd_bf[...] = wd_ref[...].astype(BF16)

        def half_step(s):
            t = 1 - s
            wait_gather(s)

            @pl.when(b >= 1)
            def _():
                wait_scatter(s)

            for j in range(tm):
                gather_row(b + 1, j, t).start(priority=j % 2)
                scatter_row(b - 1, j, t).start(priority=j % 2)
            h = _unpack_rows(hbufs[s][...]).astype(BF16)
            gu = jnp.dot(h, wgu_bf[...], preferred_element_type=F32)
            ff = gu.shape[1] // 2
            gate, up = gu[:, :ff], gu[:, ff:]
            act = (gate * jax.nn.sigmoid(gate) * up).astype(BF16)
            obufs[s][...] = _pack_rows(jnp.dot(act, wd_bf[...], preferred_element_type=F32))

            nxt = jnp.minimum(b + 1, nblocks - 1)
            is_last = jnp.logical_or(b + 1 >= nblocks, blk_n[nxt] == 0)

            @pl.when(is_last)
            def _():
                loop_rows(scatter_row, b, s)
                wait_scatter(t)
                wait_scatter(s)
                wait_gather(t)

        for s in range(2):
            pl.when(b % 2 == s)(functools.partial(half_step, s))


def _moe_plan(route, tm, t):
    ids = route[:, 0:TOP_K].astype(jnp.int32)
    flat_e = ids.reshape(-1)
    n_assign = flat_e.shape[0]
    order = jnp.argsort(flat_e).astype(jnp.int32)
    counts = jnp.sum(flat_e[:, None] == jnp.arange(N_EXPERTS, dtype=jnp.int32)[None, :], axis=0,
                     dtype=jnp.int32)
    raw_start = jnp.cumsum(counts) - counts
    nblk = (counts + tm - 1) // tm
    blk_end = jnp.cumsum(nblk)
    n_blocks = -(-(n_assign + N_EXPERTS * (tm - 1)) // tm)
    bidx = jnp.arange(n_blocks, dtype=jnp.int32)
    total = blk_end[-1]
    e_of = jnp.searchsorted(blk_end, jnp.minimum(bidx, total - 1), side='right').astype(jnp.int32)
    e_of = jnp.minimum(e_of, N_EXPERTS - 1)
    j_in = bidx - (blk_end[e_of] - nblk[e_of])
    blk_i0 = raw_start[e_of] + j_in * tm
    blk_n = jnp.where(bidx < total, jnp.clip(counts[e_of] - j_in * tm, 0, tm), 0).astype(jnp.int32)
    j = jnp.arange(tm, dtype=jnp.int32)[None, :]
    last = blk_i0 + jnp.maximum(blk_n, 1) - 1
    idx = jnp.clip(jnp.minimum(blk_i0[:, None] + j, last[:, None]), 0, n_assign - 1)
    a = order[idx]
    tok, k = a // TOP_K, a % TOP_K
    row_src = tok
    row_dst = jnp.where(j < blk_n[:, None], k * t + tok, n_assign + (bidx[:, None] % 2) * tm + j)
    pad_src = jnp.zeros((1, tm), jnp.int32)
    pad_dst = n_assign + tm + j
    row_src = jnp.concatenate([pad_src, row_src, pad_src], axis=0)
    row_dst = jnp.concatenate([pad_dst, row_dst, pad_dst], axis=0)
    return e_of, blk_n, row_src.reshape(-1), row_dst.reshape(-1).astype(jnp.int32), n_blocks


def _moe(h2p, route, w_gate_up, w_down, tm=MOE_ROWS):
    t = h2p.shape[0]
    d, ff2 = w_gate_up.shape[1], w_gate_up.shape[2]
    n_assign = t * TOP_K
    blk_e, blk_n, row_src, row_dst, n_blocks = _moe_plan(route, tm, t)
    row_buf = pltpu.VMEM((tm, PACK_SUB, LANES), jnp.uint32)
    return pl.pallas_call(
        _moe_kernel,
        out_shape=jax.ShapeDtypeStruct((n_assign + 2 * tm, PACK_SUB, LANES), jnp.uint32),
        grid_spec=pltpu.PrefetchScalarGridSpec(
            num_scalar_prefetch=4,
            grid=(n_blocks,),
            in_specs=[
                pl.BlockSpec(memory_space=pl.ANY),
                pl.BlockSpec((None, d, ff2), lambda b, be, bn, rs, rd: (be[b], 0, 0)),
                pl.BlockSpec((None, ff2 // 2, d), lambda b, be, bn, rs, rd: (be[b], 0, 0)),
            ],
            out_specs=pl.BlockSpec(memory_space=pl.ANY),
            scratch_shapes=[row_buf, row_buf, row_buf, row_buf,
                            pltpu.VMEM((d, ff2), BF16), pltpu.VMEM((ff2 // 2, d), BF16),
                            pltpu.SemaphoreType.DMA((2,)), pltpu.SemaphoreType.DMA((2,))],
        ),
        compiler_params=_params(("arbitrary",)),
        name="moe",
    )(blk_e, blk_n, row_src, row_dst, h2p, w_gate_up, w_down)


def kernel(x, c, positions, w_ada, b_ada, g_mix, w_in, b_in, attn_sinks, conv_w, conv_b, conv_ln_g,
           conv_ln_b, w_out, b_out, g_ffn, w_group_router, b_group_router, w_expert_router,
           b_expert_router, w_gate_up, w_down, g_final):
    nb, seq, d = x.shape
    t = nb * seq
    mod = _ada(c, w_ada, b_ada)
    shift1, scale1, gate1, shift2, scale2, gate2 = [mod[:, i * d:(i + 1) * d] for i in range(6)]
    x2d = x.reshape(t, d)
    q, kv, ab = _inproj(x2d, positions.reshape(t, 1), g_mix, shift1, scale1, w_in, b_in, seq)
    attn = _attn(q, kv, attn_sinks, nb, seq)
    x2 = _mix(ab, attn, x2d, conv_w, conv_b, conv_ln_g, conv_ln_b, w_out, b_out, gate1, nb, seq)
    h2, route = _router(x2, g_ffn, shift2, scale2, w_group_router, b_group_router,
                        w_expert_router, b_expert_router, seq)
    ys = _moe(h2, route, w_gate_up, w_down)
    out = _final(x2, ys, route, gate2, g_final, seq)
    return out.reshape(nb, seq, d)
```

```python
import functools
import math

import jax
import jax.numpy as jnp
from jax import lax
from jax.experimental import pallas as pl
from jax.experimental.pallas import tpu as pltpu

F32 = jnp.float32
BF16 = jnp.bfloat16

HEAD_DIM = 64
N_Q_HEADS = 16
N_KV_HEADS = 2
ATTN_WIDTH = N_Q_HEADS * HEAD_DIM
KV_WIDTH = N_KV_HEADS * HEAD_DIM
CONV_SIZE = 31
WINDOW = 128
ROPE_THETA = 10000.0
N_GROUPS = 8
EXPERTS_PER_GROUP = 8
N_EXPERTS = N_GROUPS * EXPERTS_PER_GROUP
TOP_K = 2
EXPERT_FF = 512
RMS_EPS = 1e-6
LN_EPS = 1e-5
NEG_INF = -1e30

LANES = 128
SUBLANES = 8
VMEM_LIMIT = 56 * 1024 * 1024


def _params(semantics, vmem=VMEM_LIMIT):
    return pltpu.CompilerParams(dimension_semantics=semantics, vmem_limit_bytes=vmem)


def _ada_kernel(cb_ref, w_ref, b_ref, o_ref, s_ref):
    nb, d = cb_ref.shape[0], w_ref.shape[0]
    nl = w_ref.shape[1] // LANES
    cv = cb_ref[...]
    s_ref[...] = cv * jax.nn.sigmoid(cv)

    def body(k, acc):
        r = pl.multiple_of(k * SUBLANES, SUBLANES)
        w8 = w_ref[pl.ds(r, SUBLANES), :]
        new = []
        for b in range(nb):
            sv = s_ref[b, pl.ds(r, SUBLANES), :]
            for j in range(nl):
                new.append(acc[b * nl + j] + w8[:, j * LANES:(j + 1) * LANES] * sv)
        return tuple(new)

    init = tuple(jnp.zeros((SUBLANES, LANES), F32) for _ in range(nb * nl))
    acc = lax.fori_loop(0, d // SUBLANES, body, init, unroll=4)
    for b in range(nb):
        row = jnp.concatenate(
            [jnp.sum(acc[b * nl + j], axis=0, keepdims=True) for j in range(nl)], axis=1)
        o_ref[b:b + 1, :] = row + b_ref[...]


def _ada(c, w_ada, b_ada, tn=1024):
    nb, d = c.shape
    n = w_ada.shape[1]
    cb = jnp.broadcast_to(c[:, :, None], (nb, d, LANES))
    return pl.pallas_call(
        _ada_kernel,
        out_shape=jax.ShapeDtypeStruct((nb, n), F32),
        grid=(n // tn,),
        in_specs=[
            pl.BlockSpec((nb, d, LANES), lambda j: (0, 0, 0)),
            pl.BlockSpec((d, tn), lambda j: (0, j)),
            pl.BlockSpec((1, tn), lambda j: (0, j)),
        ],
        out_specs=pl.BlockSpec((nb, tn), lambda j: (0, j)),
        scratch_shapes=[pltpu.VMEM((nb, d, LANES), F32)],
        compiler_params=_params(("arbitrary",)),
        name="ada",
    )(cb, w_ada, b_ada.reshape(1, n))


def _rmsnorm_mod(x, g, shift, scale):
    ms = jnp.mean(x * x, axis=-1, keepdims=True)
    h = (x * lax.rsqrt(ms + RMS_EPS)) * g
    return h * (1.0 + scale) + shift


def _inproj_kernel(x_ref, pos_ref, g_ref, sh_ref, sc_ref, w_ref, b_ref, invf_ref, sgn_ref,
                   q_ref, kv_ref, ab_ref):
    h = _rmsnorm_mod(x_ref[...], g_ref[...], sh_ref[0], sc_ref[0]).astype(BF16)

    ang = pos_ref[...].astype(F32) * invf_ref[...]
    cosv = jnp.cos(ang)
    sinv = jnp.sin(ang) * sgn_ref[...]
    lane = lax.broadcasted_iota(jnp.int32, (1, LANES), 1)
    first_half = (lane % HEAD_DIM) < (HEAD_DIM // 2)

    def rope(t):
        rot = jnp.where(first_half, pltpu.roll(t, LANES - HEAD_DIM // 2, 1),
                        pltpu.roll(t, HEAD_DIM // 2, 1))
        return t * cosv + rot * sinv

    def proj(c0, width):
        return (jnp.dot(h, w_ref[:, c0:c0 + width], preferred_element_type=F32)
                + b_ref[:, c0:c0 + width])

    qscale = 1.0 / math.sqrt(HEAD_DIM)
    step = 512
    for c0 in range(0, ATTN_WIDTH, step):
        z = proj(c0, step)
        for j in range(step // LANES):
            q_ref[:, c0 + j * LANES:c0 + (j + 1) * LANES] = (
                rope(z[:, j * LANES:(j + 1) * LANES]) * qscale).astype(BF16)
    z = proj(ATTN_WIDTH, 2 * KV_WIDTH)
    k = rope(z[:, :KV_WIDTH])
    v = z[:, KV_WIDTH:]
    kv_ref[:, 0 * LANES:1 * LANES] = k.astype(BF16)
    kv_ref[:, 1 * LANES:2 * LANES] = pltpu.roll(k, HEAD_DIM, 1).astype(BF16)
    kv_ref[:, 2 * LANES:3 * LANES] = v.astype(BF16)
    kv_ref[:, 3 * LANES:4 * LANES] = pltpu.roll(v, HEAD_DIM, 1).astype(BF16)
    base = ATTN_WIDTH + 2 * KV_WIDTH
    for c0 in range(0, ab_ref.shape[1], step):
        ab_ref[:, c0:c0 + step] = proj(base + c0, step).astype(BF16)


def _inproj(x2d, pos2d, g_mix, shift1, scale1, w_in, b_in, seq, tm=512):
    t, d = x2d.shape
    n = w_in.shape[1]
    nb = shift1.shape[0]
    half = HEAD_DIM // 2
    inv_freq = ROPE_THETA ** (-jnp.arange(half, dtype=F32) * 2.0 / HEAD_DIM)
    invf = jnp.tile(inv_freq, LANES // half).reshape(1, LANES)
    sgn = jnp.tile(jnp.concatenate([-jnp.ones((half,), F32), jnp.ones((half,), F32)]),
                   LANES // HEAD_DIM).reshape(1, LANES)
    steps_per_seq = seq // tm
    conv_w2 = n - ATTN_WIDTH - 2 * KV_WIDTH
    bvec = lambda i: (i // steps_per_seq, 0, 0)
    return pl.pallas_call(
        _inproj_kernel,
        out_shape=(jax.ShapeDtypeStruct((t, ATTN_WIDTH), BF16),
                   jax.ShapeDtypeStruct((t, 4 * KV_WIDTH), BF16),
                   jax.ShapeDtypeStruct((t, conv_w2), BF16)),
        grid=(t // tm,),
        in_specs=[
            pl.BlockSpec((tm, d), lambda i: (i, 0)),
            pl.BlockSpec((tm, 1), lambda i: (i, 0)),
            pl.BlockSpec((1, d), lambda i: (0, 0)),
            pl.BlockSpec((1, 1, d), bvec),
            pl.BlockSpec((1, 1, d), bvec),
            pl.BlockSpec((d, n), lambda i: (0, 0)),
            pl.BlockSpec((1, n), lambda i: (0, 0)),
            pl.BlockSpec((1, LANES), lambda i: (0, 0)),
            pl.BlockSpec((1, LANES), lambda i: (0, 0)),
        ],
        out_specs=(pl.BlockSpec((tm, ATTN_WIDTH), lambda i: (i, 0)),
                   pl.BlockSpec((tm, 4 * KV_WIDTH), lambda i: (i, 0)),
                   pl.BlockSpec((tm, conv_w2), lambda i: (i, 0))),
        compiler_params=_params(("arbitrary",)),
        name="inproj",
    )(x2d, pos2d, g_mix.reshape(1, d), shift1.reshape(nb, 1, d), scale1.reshape(nb, 1, d),
      w_in.astype(BF16), b_in.reshape(1, n), invf, sgn)


def _attn_kernel(sink_ref, q_ref, kvm_ref, kvh_ref, o_ref):
    i = pl.program_id(1)
    tq = q_ref.shape[0]
    blk = WINDOW
    heads_per_group = N_Q_HEADS // N_KV_HEADS
    pairs = heads_per_group // 2
    row = lax.broadcasted_iota(jnp.int32, (blk, 2 * blk), 0)
    col = lax.broadcasted_iota(jnp.int32, (blk, 2 * blk), 1)
    rel = row + blk - col
    band = (rel >= 0) & (rel < WINDOW)
    lane = lax.broadcasted_iota(jnp.int32, (1, LANES), 1)
    lo = lane < HEAD_DIM
    zero = jnp.zeros((), BF16)

    for j in range(tq // blk):
        if j == 0:
            kv = jnp.concatenate([kvh_ref[...], kvm_ref[0:blk, :]], axis=0)
            valid = band & ((col >= blk) | (i > 0))
        else:
            kv = kvm_ref[(j - 1) * blk:(j + 1) * blk, :]
            valid = band
        k_nat, k_swp = kv[:, 0:LANES], kv[:, LANES:2 * LANES]
        v_nat, v_swp = kv[:, 2 * LANES:3 * LANES], kv[:, 3 * LANES:4 * LANES]
        for g in range(N_KV_HEADS):
            src_lo_k, src_hi_k = (k_nat, k_swp) if g == 0 else (k_swp, k_nat)
            src_lo_v, src_hi_v = (v_nat, v_swp) if g == 0 else (v_swp, v_nat)
            k_even = jnp.where(lo, src_lo_k, zero)
            k_odd = jnp.where(lo, zero, src_hi_k)
            v_even = jnp.where(lo, src_lo_v, zero)
            v_odd = jnp.where(lo, zero, src_hi_v)
            qs = jnp.concatenate(
                [q_ref[j * blk:(j + 1) * blk, (g * pairs + p) * LANES:(g * pairs + p + 1) * LANES]
                 for p in range(pairs)], axis=0)
            nt = (((1,), (1,)), ((), ()))
            s_even = lax.dot_general(qs, k_even, nt, preferred_element_type=F32)
            s_odd = lax.dot_general(qs, k_odd, nt, preferred_element_type=F32)
            probs, rinv = [], []
            for s_all, parity in ((s_even, 0), (s_odd, 1)):
                ps, rs = [], []
                for p in range(pairs):
                    sink = sink_ref[g * heads_per_group + 2 * p + parity]
                    s = jnp.where(valid, s_all[p * blk:(p + 1) * blk, :], NEG_INF)
                    m = jnp.maximum(jnp.max(s, axis=-1, keepdims=True), sink)
                    e = jnp.exp(s - m)
                    den = jnp.sum(e, axis=-1, keepdims=True) + jnp.exp(sink - m)
                    ps.append(e.astype(BF16))
                    rs.append(1.0 / den)
                probs.append(jnp.concatenate(ps, axis=0))
                rinv.append(rs)
            o = (jnp.dot(probs[0], v_even, preferred_element_type=F32)
                 + jnp.dot(probs[1], v_odd, preferred_element_type=F32))
            for p in range(pairs):
                scale = jnp.where(lo, rinv[0][p], rinv[1][p])
                c = (g * pairs + p) * LANES
                o_ref[j * blk:(j + 1) * blk, c:c + LANES] = (
                    o[p * blk:(p + 1) * blk, :] * scale).astype(BF16)


def _attn(q, kv, sinks, nb, seq, tq=512):
    blk = WINDOW
    q3 = q.reshape(nb, seq, ATTN_WIDTH)
    kv3 = kv.reshape(nb, seq, 4 * KV_WIDTH)
    sub = tq // blk
    out = pl.pallas_call(
        _attn_kernel,
        out_shape=jax.ShapeDtypeStruct((nb, seq, ATTN_WIDTH), BF16),
        grid=(nb, seq // tq),
        in_specs=[
            pl.BlockSpec(memory_space=pltpu.SMEM),
            pl.BlockSpec((None, tq, ATTN_WIDTH), lambda b, i: (b, i, 0)),
            pl.BlockSpec((None, tq, 4 * KV_WIDTH), lambda b, i: (b, i, 0)),
            pl.BlockSpec((None, blk, 4 * KV_WIDTH), lambda b, i: (b, jnp.maximum(i * sub - 1, 0), 0)),
        ],
        out_specs=pl.BlockSpec((None, tq, ATTN_WIDTH), lambda b, i: (b, i, 0)),
        compiler_params=_params(("arbitrary", "arbitrary")),
        name="attn",
    )(sinks.astype(F32), q3, kv3, kv3)
    return out.reshape(nb * seq, ATTN_WIDTH)


CONV_HALO = 32


def _mix_kernel(ab_ref, abh_ref, attn_ref, x_ref, wc_ref, cb_ref, lg_ref, lb_ref, wo_ref, bo_ref,
                gate_ref, o_ref, us_ref, y_ref):
    i = pl.program_id(1)
    tt = x_ref.shape[0]
    cw = y_ref.shape[1]

    def glu(ab):
        a = ab[:, :cw].astype(F32)
        b = ab[:, cw:].astype(F32)
        return a * jax.nn.sigmoid(b)

    halo = glu(abh_ref[...])
    us_ref[0, 0:CONV_HALO, :] = jnp.where(i > 0, halo, 0.0)
    us_ref[0, CONV_HALO:, :] = glu(ab_ref[...])
    n = tt + CONV_HALO - SUBLANES
    for s in range(1, SUBLANES):
        us_ref[s, 0:n, :] = us_ref[0, s:s + n, :]

    off = CONV_HALO - (CONV_SIZE - 1)

    def conv_lanes(c, carry):
        l0 = pl.multiple_of(c * LANES, LANES)
        acc = jnp.zeros((tt, LANES), F32)
        for k in range(CONV_SIZE):
            o = off + k
            a0 = (o // SUBLANES) * SUBLANES
            acc = acc + (us_ref[o % SUBLANES, a0:a0 + tt, pl.ds(l0, LANES)]
                         * wc_ref[k:k + 1, pl.ds(l0, LANES)])
        y_ref[:, pl.ds(l0, LANES)] = acc
        return carry

    lax.fori_loop(0, cw // LANES, conv_lanes, 0)

    y = y_ref[...] + cb_ref[...]
    mu = jnp.mean(y, axis=-1, keepdims=True)
    yc = y - mu
    var = jnp.mean(yc * yc, axis=-1, keepdims=True)
    yn = yc * lax.rsqrt(var + LN_EPS) * lg_ref[...] + lb_ref[...]
    conv = (yn * jax.nn.sigmoid(yn)).astype(BF16)

    aw = attn_ref.shape[1]
    mixed = (jnp.dot(attn_ref[...], wo_ref[0:aw, :], preferred_element_type=F32)
             + jnp.dot(conv, wo_ref[aw:, :], preferred_element_type=F32) + bo_ref[...])
    o_ref[...] = x_ref[...] + gate_ref[0] * mixed


def _mix(ab, attn, x2d, conv_w, conv_b, ln_g, ln_b, w_out, b_out, gate1, nb, seq, tt=256):
    t, d = x2d.shape
    cw = conv_w.shape[1]
    aw = attn.shape[1]
    ab3 = ab.reshape(nb, seq, 2 * cw)
    attn3 = attn.reshape(nb, seq, aw)
    x3 = x2d.reshape(nb, seq, d)
    hsub = tt // CONV_HALO
    out = pl.pallas_call(
        _mix_kernel,
        out_shape=jax.ShapeDtypeStruct((nb, seq, d), F32),
        grid=(nb, seq // tt),
        in_specs=[
            pl.BlockSpec((None, tt, 2 * cw), lambda b, i: (b, i, 0)),
            pl.BlockSpec((None, CONV_HALO, 2 * cw), lambda b, i: (b, jnp.maximum(i * hsub - 1, 0), 0)),
            pl.BlockSpec((None, tt, aw), lambda b, i: (b, i, 0)),
            pl.BlockSpec((None, tt, d), lambda b, i: (b, i, 0)),
            pl.BlockSpec((CONV_SIZE, cw), lambda b, i: (0, 0)),
            pl.BlockSpec((1, cw), lambda b, i: (0, 0)),
            pl.BlockSpec((1, cw), lambda b, i: (0, 0)),
            pl.BlockSpec((1, cw), lambda b, i: (0, 0)),
            pl.BlockSpec((d, d), lambda b, i: (0, 0)),
            pl.BlockSpec((1, d), lambda b, i: (0, 0)),
            pl.BlockSpec((1, 1, d), lambda b, i: (b, 0, 0)),
        ],
        out_specs=pl.BlockSpec((None, tt, d), lambda b, i: (b, i, 0)),
        scratch_shapes=[pltpu.VMEM((SUBLANES, CONV_HALO + tt, cw), F32), pltpu.VMEM((tt, cw), F32)],
        compiler_params=_params(("arbitrary", "arbitrary")),
        name="mix",
    )(ab3, ab3, attn3, x3, conv_w, conv_b.reshape(1, cw), ln_g.reshape(1, cw), ln_b.reshape(1, cw),
      w_out.astype(BF16), b_out.reshape(1, d), gate1.reshape(nb, 1, d))
    return out.reshape(t, d)


def _split_bf16(a):
    hi = a.astype(BF16)
    lo = (a - hi.astype(F32)).astype(BF16)
    return hi, lo


PACK_SUB = SUBLANES
HI_MASK = 0xFFFF0000


def _pack_rows(v):
    half = v.shape[1] // 2
    assert half == PACK_SUB * LANES
    bits = lax.bitcast_convert_type(v.astype(BF16).astype(F32), jnp.uint32)
    word = (bits[:, :half] >> 16) | (bits[:, half:] & jnp.uint32(HI_MASK))
    planes = jnp.stack([word[:, s * LANES:(s + 1) * LANES] for s in range(PACK_SUB)], axis=0)
    return pltpu.einshape("smr->msr", planes)


def _unpack_rows(p):
    planes = pltpu.einshape("msr->smr", p)
    lo = [lax.bitcast_convert_type(planes[s] << 16, F32) for s in range(PACK_SUB)]
    hi = [lax.bitcast_convert_type(planes[s] & jnp.uint32(HI_MASK), F32) for s in range(PACK_SUB)]
    return jnp.concatenate(lo + hi, axis=1)


def _router_kernel(x_ref, g_ref, sh_ref, sc_ref, whi_ref, wlo_ref, b_ref, h_ref, r_ref):
    h = _rmsnorm_mod(x_ref[...], g_ref[...], sh_ref[0], sc_ref[0])
    h_ref[...] = _pack_rows(h)
    hi, lo = _split_bf16(h)
    logits = (jnp.dot(hi, whi_ref[...], preferred_element_type=F32)
              + jnp.dot(hi, wlo_ref[...], preferred_element_type=F32)
              + jnp.dot(lo, whi_ref[...], preferred_element_type=F32)) + b_ref[...]
    tm = logits.shape[0]
    lane = lax.broadcasted_iota(jnp.int32, (tm, LANES), 1)
    big = jnp.int32(LANES)

    def first_argmax(v):
        m = jnp.max(v, axis=-1, keepdims=True)
        idx = jnp.min(jnp.where(v == m, lane, big), axis=-1, keepdims=True)
        return m, idx

    is_group = lane < N_GROUPS
    gl = jnp.where(is_group, logits, NEG_INF)
    gmax, gsel = first_argmax(gl)
    p_g = 1.0 / jnp.sum(jnp.where(is_group, jnp.exp(gl - gmax), 0.0), axis=-1, keepdims=True)
    e_lo = N_GROUPS + gsel * EXPERTS_PER_GROUP
    in_group = (lane >= e_lo) & (lane < e_lo + EXPERTS_PER_GROUP)
    el = jnp.where(in_group, logits, NEG_INF)
    t1, i1 = first_argmax(el)
    el2 = jnp.where(lane == i1, NEG_INF, el)
    t2, i2 = first_argmax(el2)
    e2 = jnp.exp(t2 - t1)
    w1 = p_g / (1.0 + e2)
    w2 = p_g * e2 / (1.0 + e2)
    id1 = (i1 - N_GROUPS).astype(F32)
    id2 = (i2 - N_GROUPS).astype(F32)
    r_ref[...] = jnp.where(lane == 0, id1, jnp.where(lane == 1, id2,
                           jnp.where(lane == 2, w1, jnp.where(lane == 3, w2, 0.0))))


def _router(x2, g_ffn, shift2, scale2, w_gr, b_gr, w_er, b_er, seq, tm=512):
    t, d = x2.shape
    nb = shift2.shape[0]
    w = jnp.zeros((d, LANES), F32).at[:, :N_GROUPS].set(w_gr).at[:, N_GROUPS:N_GROUPS + N_EXPERTS].set(w_er)
    bias = jnp.zeros((1, LANES), F32).at[0, :N_GROUPS].set(b_gr).at[0, N_GROUPS:N_GROUPS + N_EXPERTS].set(b_er)
    whi = w.astype(BF16)
    wlo = (w - whi.astype(F32)).astype(BF16)
    steps_per_seq = seq // tm
    bvec = lambda i: (i // steps_per_seq, 0, 0)
    return pl.pallas_call(
        _router_kernel,
        out_shape=(jax.ShapeDtypeStruct((t, PACK_SUB, LANES), jnp.uint32),
                   jax.ShapeDtypeStruct((t, LANES), F32)),
        grid=(t // tm,),
        in_specs=[
            pl.BlockSpec((tm, d), lambda i: (i, 0)),
            pl.BlockSpec((1, d), lambda i: (0, 0)),
            pl.BlockSpec((1, 1, d), bvec),
            pl.BlockSpec((1, 1, d), bvec),
            pl.BlockSpec((d, LANES), lambda i: (0, 0)),
            pl.BlockSpec((d, LANES), lambda i: (0, 0)),
            pl.BlockSpec((1, LANES), lambda i: (0, 0)),
        ],
        out_specs=(pl.BlockSpec((tm, PACK_SUB, LANES), lambda i: (i, 0, 0)),
                   pl.BlockSpec((tm, LANES), lambda i: (i, 0))),
        compiler_params=_params(("arbitrary",)),
        name="router",
    )(x2, g_ffn.reshape(1, d), shift2.reshape(nb, 1, d), scale2.reshape(nb, 1, d), whi, wlo, bias)


def _final_kernel(x_ref, y0_ref, y1_ref, r_ref, gate_ref, g_ref, o_ref):
    w0 = r_ref[:, 2:3]
    w1 = r_ref[:, 3:4]
    ffn = w0 * _unpack_rows(y0_ref[...]) + w1 * _unpack_rows(y1_ref[...])
    x = x_ref[...] + gate_ref[0] * ffn
    ms = jnp.mean(x * x, axis=-1, keepdims=True)
    o_ref[...] = (x * lax.rsqrt(ms + RMS_EPS)) * g_ref[...]


def _final(x2, ys, route, gate2, g_final, seq, tm=512):
    t, d = x2.shape
    nb = gate2.shape[0]
    steps_per_seq = seq // tm
    return pl.pallas_call(
        _final_kernel,
        out_shape=jax.ShapeDtypeStruct((t, d), F32),
        grid=(t // tm,),
        in_specs=[
            pl.BlockSpec((tm, d), lambda i: (i, 0)),
            pl.BlockSpec((tm, PACK_SUB, LANES), lambda i: (i, 0, 0)),
            pl.BlockSpec((tm, PACK_SUB, LANES), lambda i: (i + t // tm, 0, 0)),
            pl.BlockSpec((tm, LANES), lambda i: (i, 0)),
            pl.BlockSpec((1, 1, d), lambda i: (i // steps_per_seq, 0, 0)),
            pl.BlockSpec((1, d), lambda i: (0, 0)),
        ],
        out_specs=pl.BlockSpec((tm, d), lambda i: (i, 0)),
        compiler_params=_params(("arbitrary",)),
        name="final",
    )(x2, ys, ys, route, gate2.reshape(nb, 1, d), g_final.reshape(1, d))


MOE_ROWS = 256
BLK_LEAD = 2


def _moe_kernel(blk_e, blk_n, blk_i0, src_tok, dst_row, h_hbm, wgu_ref, wd_ref, ys_hbm,
                h0, h1, o0, o1, wgu_bf, wd_bf, gsem, ssem):
    b = pl.program_id(0)
    tm = h0.shape[0]
    dump0 = ys_hbm.shape[0] - 2 * tm
    hbufs, obufs = (h0, h1), (o0, o1)

    def count(blk):
        return blk_n[blk + BLK_LEAD]

    def first(blk):
        return blk_i0[blk + BLK_LEAD]

    def gather_row(i0, j, s):
        return pltpu.make_async_copy(h_hbm.at[src_tok[i0 + j]], hbufs[s].at[j], gsem.at[s])

    def scatter_row(i0, n_rows, j, s):
        row = jnp.where(j < n_rows, dst_row[i0 + j], dump0 + s * tm + j)
        return pltpu.make_async_copy(obufs[s].at[j], ys_hbm.at[row], ssem.at[s])

    def wait_gather(s):
        pltpu.make_async_copy(h_hbm.at[pl.ds(0, tm)], hbufs[s], gsem.at[s]).wait()

    def wait_scatter(s):
        pltpu.make_async_copy(obufs[s], ys_hbm.at[pl.ds(0, tm)], ssem.at[s]).wait()

    @pl.when(b == 0)
    def _():
        for s in range(2):
            obufs[s][...] = jnp.zeros(obufs[s].shape, obufs[s].dtype)
            dump = pltpu.make_async_copy(obufs[s], ys_hbm.at[pl.ds(dump0 + s * tm, tm)], ssem.at[s])
            dump.start()
            dump.wait()
        i0 = first(0)

        def body(j, carry):
            gather_row(i0, j, 0).start()
            return carry
        lax.fori_loop(0, tm, body, 0, unroll=8)

    @pl.when(count(b) > 0)
    def _():
        prev_e = blk_e[jnp.maximum(b - 1, 0)]

        @pl.when(jnp.logical_or(b == 0, blk_e[b] != prev_e))
        def _():
            wgu_bf[...] = wgu_ref[...].astype(BF16)
            wd_bf[...] = wd_ref[...].astype(BF16)

        def half_step(s):
            t = 1 - s
            wait_gather(s)

            @pl.when(b >= 1)
            def _():
                wait_scatter(s)

            i0_next, i0_prev, n_prev = first(b + 1), first(b - 1), count(b - 1)
            for j in range(tm):
                gather_row(i0_next, j, t).start(priority=j % 2)
                scatter_row(i0_prev, n_prev, j, t).start(priority=j % 2)
            h = _unpack_rows(hbufs[s][...]).astype(BF16)
            gu = jnp.dot(h, wgu_bf[...], preferred_element_type=F32)
            ff = gu.shape[1] // 2
            gate, up = gu[:, :ff], gu[:, ff:]
            act = (gate * jax.nn.sigmoid(gate) * up).astype(BF16)
            obufs[s][...] = _pack_rows(jnp.dot(act, wd_bf[...], preferred_element_type=F32))

            @pl.when(count(b + 1) == 0)
            def _():
                i0, n_rows = first(b), count(b)

                def body(j, carry):
                    scatter_row(i0, n_rows, j, s).start()
                    return carry
                lax.fori_loop(0, tm, body, 0, unroll=8)
                wait_scatter(t)
                wait_scatter(s)
                wait_gather(t)

        for s in range(2):
            pl.when(b % 2 == s)(functools.partial(half_step, s))


def _moe_plan(route, tm, t):
    ids = route[:, 0:TOP_K].astype(jnp.int32)
    flat_e = ids.reshape(-1)
    n_assign = flat_e.shape[0]
    order = jnp.argsort(flat_e).astype(jnp.int32)
    counts = jnp.sum(flat_e[:, None] == jnp.arange(N_EXPERTS, dtype=jnp.int32)[None, :], axis=0,
                     dtype=jnp.int32)
    raw_start = jnp.cumsum(counts) - counts
    nblk = (counts + tm - 1) // tm
    blk_end = jnp.cumsum(nblk)
    n_blocks = -(-(n_assign + N_EXPERTS * (tm - 1)) // tm)
    bidx = jnp.arange(n_blocks, dtype=jnp.int32)
    total = blk_end[-1]
    e_of = jnp.searchsorted(blk_end, jnp.minimum(bidx, total - 1), side='right').astype(jnp.int32)
    e_of = jnp.minimum(e_of, N_EXPERTS - 1)
    j_in = bidx - (blk_end[e_of] - nblk[e_of])
    blk_i0 = raw_start[e_of] + j_in * tm
    blk_n = jnp.where(bidx < total, jnp.clip(counts[e_of] - j_in * tm, 0, tm), 0).astype(jnp.int32)
    blk_i0 = jnp.where(bidx < total, blk_i0, 0).astype(jnp.int32)
    lead, trail = jnp.zeros((BLK_LEAD,), jnp.int32), jnp.zeros((1,), jnp.int32)
    blk_n = jnp.concatenate([lead, blk_n, trail])
    blk_i0 = jnp.concatenate([lead, blk_i0, trail])
    tok, k = order // TOP_K, order % TOP_K
    spare = jnp.arange(tm, dtype=jnp.int32) % t
    src_tok = jnp.concatenate([tok, spare])
    dst_row = jnp.concatenate([k * t + tok, spare])
    return e_of, blk_n, blk_i0, src_tok, dst_row, n_blocks


def _moe(h2p, route, w_gate_up, w_down, tm=MOE_ROWS):
    t = h2p.shape[0]
    d, ff2 = w_gate_up.shape[1], w_gate_up.shape[2]
    n_assign = t * TOP_K
    blk_e, blk_n, blk_i0, src_tok, dst_row, n_blocks = _moe_plan(route, tm, t)
    row_buf = pltpu.VMEM((tm, PACK_SUB, LANES), jnp.uint32)
    w_map = lambda b, be, bn, bi, st, dr: (be[b], 0, 0)
    return pl.pallas_call(
        _moe_kernel,
        out_shape=jax.ShapeDtypeStruct((n_assign + 2 * tm, PACK_SUB, LANES), jnp.uint32),
        grid_spec=pltpu.PrefetchScalarGridSpec(
            num_scalar_prefetch=5,
            grid=(n_blocks,),
            in_specs=[
                pl.BlockSpec(memory_space=pl.ANY),
                pl.BlockSpec((None, d, ff2), w_map),
                pl.BlockSpec((None, ff2 // 2, d), w_map),
            ],
            out_specs=pl.BlockSpec(memory_space=pl.ANY),
            scratch_shapes=[row_buf, row_buf, row_buf, row_buf,
                            pltpu.VMEM((d, ff2), BF16), pltpu.VMEM((ff2 // 2, d), BF16),
                            pltpu.SemaphoreType.DMA((2,)), pltpu.SemaphoreType.DMA((2,))],
        ),
        compiler_params=_params(("arbitrary",)),
        name="moe",
    )(blk_e, blk_n, blk_i0, src_tok, dst_row, h2p, w_gate_up, w_down)


def kernel(x, c, positions, w_ada, b_ada, g_mix, w_in, b_in, attn_sinks, conv_w, conv_b, conv_ln_g,
           conv_ln_b, w_out, b_out, g_ffn, w_group_router, b_group_router, w_expert_router,
           b_expert_router, w_gate_up, w_down, g_final):
    nb, seq, d = x.shape
    t = nb * seq
    mod = _ada(c, w_ada, b_ada)
    shift1, scale1, gate1, shift2, scale2, gate2 = [mod[:, i * d:(i + 1) * d] for i in range(6)]
    x2d = x.reshape(t, d)
    q, kv, ab = _inproj(x2d, positions.reshape(t, 1), g_mix, shift1, scale1, w_in, b_in, seq)
    attn = _attn(q, kv, attn_sinks, nb, seq)
    x2 = _mix(ab, attn, x2d, conv_w, conv_b, conv_ln_g, conv_ln_b, w_out, b_out, gate1, nb, seq)
    h2, route = _router(x2, g_ffn, shift2, scale2, w_group_router, b_group_router,
                        w_expert_router, b_expert_router, seq)
    ys = _moe(h2, route, w_gate_up, w_down)
    out = _final(x2, ys, route, gate2, g_final, seq)
    return out.reshape(nb, seq, d)
```

```python
import functools
import math

import jax
import jax.numpy as jnp
from jax import lax
from jax.experimental import pallas as pl
from jax.experimental.pallas import tpu as pltpu

F32 = jnp.float32
BF16 = jnp.bfloat16

HEAD_DIM = 64
N_Q_HEADS = 16
N_KV_HEADS = 2
ATTN_WIDTH = N_Q_HEADS * HEAD_DIM
KV_WIDTH = N_KV_HEADS * HEAD_DIM
CONV_SIZE = 31
WINDOW = 128
ROPE_THETA = 10000.0
N_GROUPS = 8
EXPERTS_PER_GROUP = 8
N_EXPERTS = N_GROUPS * EXPERTS_PER_GROUP
TOP_K = 2
EXPERT_FF = 512
RMS_EPS = 1e-6
LN_EPS = 1e-5
NEG_INF = -1e30

LANES = 128
SUBLANES = 8
VMEM_LIMIT = 56 * 1024 * 1024


def _params(semantics, vmem=VMEM_LIMIT):
    return pltpu.CompilerParams(dimension_semantics=semantics, vmem_limit_bytes=vmem)


def _ada_kernel(cb_ref, w_ref, b_ref, o_ref, s_ref):
    nb, d = cb_ref.shape[0], w_ref.shape[0]
    nl = w_ref.shape[1] // LANES
    cv = cb_ref[...]
    s_ref[...] = cv * jax.nn.sigmoid(cv)

    def body(k, acc):
        r = pl.multiple_of(k * SUBLANES, SUBLANES)
        w8 = w_ref[pl.ds(r, SUBLANES), :]
        new = []
        for b in range(nb):
            sv = s_ref[b, pl.ds(r, SUBLANES), :]
            for j in range(nl):
                new.append(acc[b * nl + j] + w8[:, j * LANES:(j + 1) * LANES] * sv)
        return tuple(new)

    init = tuple(jnp.zeros((SUBLANES, LANES), F32) for _ in range(nb * nl))
    acc = lax.fori_loop(0, d // SUBLANES, body, init, unroll=4)
    for b in range(nb):
        row = jnp.concatenate(
            [jnp.sum(acc[b * nl + j], axis=0, keepdims=True) for j in range(nl)], axis=1)
        o_ref[b:b + 1, :] = row + b_ref[...]


def _ada(c, w_ada, b_ada, tn=1024):
    nb, d = c.shape
    n = w_ada.shape[1]
    cb = jnp.broadcast_to(c[:, :, None], (nb, d, LANES))
    return pl.pallas_call(
        _ada_kernel,
        out_shape=jax.ShapeDtypeStruct((nb, n), F32),
        grid=(n // tn,),
        in_specs=[
            pl.BlockSpec((nb, d, LANES), lambda j: (0, 0, 0)),
            pl.BlockSpec((d, tn), lambda j: (0, j)),
            pl.BlockSpec((1, tn), lambda j: (0, j)),
        ],
        out_specs=pl.BlockSpec((nb, tn), lambda j: (0, j)),
        scratch_shapes=[pltpu.VMEM((nb, d, LANES), F32)],
        compiler_params=_params(("arbitrary",)),
        name="ada",
    )(cb, w_ada, b_ada.reshape(1, n))


def _rmsnorm_mod(x, g, shift, scale):
    ms = jnp.mean(x * x, axis=-1, keepdims=True)
    h = (x * lax.rsqrt(ms + RMS_EPS)) * g
    return h * (1.0 + scale) + shift


def _inproj_kernel(x_ref, pos_ref, g_ref, sh_ref, sc_ref, w_ref, b_ref, invf_ref, sgn_ref,
                   q_ref, kv_ref, ab_ref):
    h = _rmsnorm_mod(x_ref[...], g_ref[...], sh_ref[0], sc_ref[0]).astype(BF16)

    ang = pos_ref[...].astype(F32) * invf_ref[...]
    cosv = jnp.cos(ang)
    sinv = jnp.sin(ang) * sgn_ref[...]
    lane = lax.broadcasted_iota(jnp.int32, (1, LANES), 1)
    first_half = (lane % HEAD_DIM) < (HEAD_DIM // 2)

    def rope(t):
        rot = jnp.where(first_half, pltpu.roll(t, LANES - HEAD_DIM // 2, 1),
                        pltpu.roll(t, HEAD_DIM // 2, 1))
        return t * cosv + rot * sinv

    def proj(c0, width):
        return (jnp.dot(h, w_ref[:, c0:c0 + width], preferred_element_type=F32)
                + b_ref[:, c0:c0 + width])

    qscale = 1.0 / math.sqrt(HEAD_DIM)
    step = 512
    for c0 in range(0, ATTN_WIDTH, step):
        z = proj(c0, step)
        for j in range(step // LANES):
            q_ref[:, c0 + j * LANES:c0 + (j + 1) * LANES] = (
                rope(z[:, j * LANES:(j + 1) * LANES]) * qscale).astype(BF16)
    z = proj(ATTN_WIDTH, 2 * KV_WIDTH)
    k = rope(z[:, :KV_WIDTH])
    v = z[:, KV_WIDTH:]
    kv_ref[:, 0 * LANES:1 * LANES] = k.astype(BF16)
    kv_ref[:, 1 * LANES:2 * LANES] = pltpu.roll(k, HEAD_DIM, 1).astype(BF16)
    kv_ref[:, 2 * LANES:3 * LANES] = v.astype(BF16)
    kv_ref[:, 3 * LANES:4 * LANES] = pltpu.roll(v, HEAD_DIM, 1).astype(BF16)
    base = ATTN_WIDTH + 2 * KV_WIDTH
    for c0 in range(0, ab_ref.shape[1], step):
        ab_ref[:, c0:c0 + step] = proj(base + c0, step).astype(BF16)


def _inproj(x2d, pos2d, g_mix, shift1, scale1, w_in, b_in, seq, tm=512):
    t, d = x2d.shape
    n = w_in.shape[1]
    nb = shift1.shape[0]
    half = HEAD_DIM // 2
    inv_freq = ROPE_THETA ** (-jnp.arange(half, dtype=F32) * 2.0 / HEAD_DIM)
    invf = jnp.tile(inv_freq, LANES // half).reshape(1, LANES)
    sgn = jnp.tile(jnp.concatenate([-jnp.ones((half,), F32), jnp.ones((half,), F32)]),
                   LANES // HEAD_DIM).reshape(1, LANES)
    steps_per_seq = seq // tm
    conv_w2 = n - ATTN_WIDTH - 2 * KV_WIDTH
    bvec = lambda i: (i // steps_per_seq, 0, 0)
    return pl.pallas_call(
        _inproj_kernel,
        out_shape=(jax.ShapeDtypeStruct((t, ATTN_WIDTH), BF16),
                   jax.ShapeDtypeStruct((t, 4 * KV_WIDTH), BF16),
                   jax.ShapeDtypeStruct((t, conv_w2), BF16)),
        grid=(t // tm,),
        in_specs=[
            pl.BlockSpec((tm, d), lambda i: (i, 0)),
            pl.BlockSpec((tm, 1), lambda i: (i, 0)),
            pl.BlockSpec((1, d), lambda i: (0, 0)),
            pl.BlockSpec((1, 1, d), bvec),
            pl.BlockSpec((1, 1, d), bvec),
            pl.BlockSpec((d, n), lambda i: (0, 0)),
            pl.BlockSpec((1, n), lambda i: (0, 0)),
            pl.BlockSpec((1, LANES), lambda i: (0, 0)),
            pl.BlockSpec((1, LANES), lambda i: (0, 0)),
        ],
        out_specs=(pl.BlockSpec((tm, ATTN_WIDTH), lambda i: (i, 0)),
                   pl.BlockSpec((tm, 4 * KV_WIDTH), lambda i: (i, 0)),
                   pl.BlockSpec((tm, conv_w2), lambda i: (i, 0))),
        compiler_params=_params(("arbitrary",)),
        name="inproj",
    )(x2d, pos2d, g_mix.reshape(1, d), shift1.reshape(nb, 1, d), scale1.reshape(nb, 1, d),
      w_in.astype(BF16), b_in.reshape(1, n), invf, sgn)


def _attn_kernel(sink_ref, q_ref, kvm_ref, kvh_ref, o_ref):
    i = pl.program_id(1)
    tq = q_ref.shape[0]
    blk = WINDOW
    heads_per_group = N_Q_HEADS // N_KV_HEADS
    pairs = heads_per_group // 2
    row = lax.broadcasted_iota(jnp.int32, (blk, 2 * blk), 0)
    col = lax.broadcasted_iota(jnp.int32, (blk, 2 * blk), 1)
    rel = row + blk - col
    band = (rel >= 0) & (rel < WINDOW)
    lane = lax.broadcasted_iota(jnp.int32, (1, LANES), 1)
    lo = lane < HEAD_DIM
    zero = jnp.zeros((), BF16)

    for j in range(tq // blk):
        if j == 0:
            kv = jnp.concatenate([kvh_ref[...], kvm_ref[0:blk, :]], axis=0)
            valid = band & ((col >= blk) | (i > 0))
        else:
            kv = kvm_ref[(j - 1) * blk:(j + 1) * blk, :]
            valid = band
        k_nat, k_swp = kv[:, 0:LANES], kv[:, LANES:2 * LANES]
        v_nat, v_swp = kv[:, 2 * LANES:3 * LANES], kv[:, 3 * LANES:4 * LANES]
        for g in range(N_KV_HEADS):
            src_lo_k, src_hi_k = (k_nat, k_swp) if g == 0 else (k_swp, k_nat)
            src_lo_v, src_hi_v = (v_nat, v_swp) if g == 0 else (v_swp, v_nat)
            k_even = jnp.where(lo, src_lo_k, zero)
            k_odd = jnp.where(lo, zero, src_hi_k)
            v_even = jnp.where(lo, src_lo_v, zero)
            v_odd = jnp.where(lo, zero, src_hi_v)
            qs = jnp.concatenate(
                [q_ref[j * blk:(j + 1) * blk, (g * pairs + p) * LANES:(g * pairs + p + 1) * LANES]
                 for p in range(pairs)], axis=0)
            nt = (((1,), (1,)), ((), ()))
            s_even = lax.dot_general(qs, k_even, nt, preferred_element_type=F32)
            s_odd = lax.dot_general(qs, k_odd, nt, preferred_element_type=F32)
            probs, rinv = [], []
            for s_all, parity in ((s_even, 0), (s_odd, 1)):
                ps, rs = [], []
                for p in range(pairs):
                    sink = sink_ref[g * heads_per_group + 2 * p + parity]
                    s = jnp.where(valid, s_all[p * blk:(p + 1) * blk, :], NEG_INF)
                    m = jnp.maximum(jnp.max(s, axis=-1, keepdims=True), sink)
                    e = jnp.exp(s - m)
                    den = jnp.sum(e, axis=-1, keepdims=True) + jnp.exp(sink - m)
                    ps.append(e.astype(BF16))
                    rs.append(1.0 / den)
                probs.append(jnp.concatenate(ps, axis=0))
                rinv.append(rs)
            o = (jnp.dot(probs[0], v_even, preferred_element_type=F32)
                 + jnp.dot(probs[1], v_odd, preferred_element_type=F32))
            for p in range(pairs):
                scale = jnp.where(lo, rinv[0][p], rinv[1][p])
                c = (g * pairs + p) * LANES
                o_ref[j * blk:(j + 1) * blk, c:c + LANES] = (
                    o[p * blk:(p + 1) * blk, :] * scale).astype(BF16)


def _attn(q, kv, sinks, nb, seq, tq=512):
    blk = WINDOW
    q3 = q.reshape(nb, seq, ATTN_WIDTH)
    kv3 = kv.reshape(nb, seq, 4 * KV_WIDTH)
    sub = tq // blk
    out = pl.pallas_call(
        _attn_kernel,
        out_shape=jax.ShapeDtypeStruct((nb, seq, ATTN_WIDTH), BF16),
        grid=(nb, seq // tq),
        in_specs=[
            pl.BlockSpec(memory_space=pltpu.SMEM),
            pl.BlockSpec((None, tq, ATTN_WIDTH), lambda b, i: (b, i, 0)),
            pl.BlockSpec((None, tq, 4 * KV_WIDTH), lambda b, i: (b, i, 0)),
            pl.BlockSpec((None, blk, 4 * KV_WIDTH), lambda b, i: (b, jnp.maximum(i * sub - 1, 0), 0)),
        ],
        out_specs=pl.BlockSpec((None, tq, ATTN_WIDTH), lambda b, i: (b, i, 0)),
        compiler_params=_params(("arbitrary", "arbitrary")),
        name="attn",
    )(sinks.astype(F32), q3, kv3, kv3)
    return out.reshape(nb * seq, ATTN_WIDTH)


CONV_HALO = 32


def _mix_kernel(ab_ref, abh_ref, attn_ref, x_ref, wc_ref, cb_ref, lg_ref, lb_ref, wo_ref, bo_ref,
                gate_ref, o_ref, us_ref, y_ref):
    i = pl.program_id(1)
    tt = x_ref.shape[0]
    cw = y_ref.shape[1]

    def glu(ab):
        a = ab[:, :cw].astype(F32)
        b = ab[:, cw:].astype(F32)
        return a * jax.nn.sigmoid(b)

    halo = glu(abh_ref[...])
    us_ref[0, 0:CONV_HALO, :] = jnp.where(i > 0, halo, 0.0)
    us_ref[0, CONV_HALO:, :] = glu(ab_ref[...])
    n = tt + CONV_HALO - SUBLANES
    for s in range(1, SUBLANES):
        us_ref[s, 0:n, :] = us_ref[0, s:s + n, :]

    off = CONV_HALO - (CONV_SIZE - 1)

    def conv_lanes(c, carry):
        l0 = pl.multiple_of(c * LANES, LANES)
        acc = jnp.zeros((tt, LANES), F32)
        for k in range(CONV_SIZE):
            o = off + k
            a0 = (o // SUBLANES) * SUBLANES
            acc = acc + (us_ref[o % SUBLANES, a0:a0 + tt, pl.ds(l0, LANES)]
                         * wc_ref[k:k + 1, pl.ds(l0, LANES)])
        y_ref[:, pl.ds(l0, LANES)] = acc
        return carry

    lax.fori_loop(0, cw // LANES, conv_lanes, 0)

    y = y_ref[...] + cb_ref[...]
    mu = jnp.mean(y, axis=-1, keepdims=True)
    yc = y - mu
    var = jnp.mean(yc * yc, axis=-1, keepdims=True)
    yn = yc * lax.rsqrt(var + LN_EPS) * lg_ref[...] + lb_ref[...]
    conv = (yn * jax.nn.sigmoid(yn)).astype(BF16)

    aw = attn_ref.shape[1]
    mixed = (jnp.dot(attn_ref[...], wo_ref[0:aw, :], preferred_element_type=F32)
             + jnp.dot(conv, wo_ref[aw:, :], preferred_element_type=F32) + bo_ref[...])
    o_ref[...] = x_ref[...] + gate_ref[0] * mixed


def _mix(ab, attn, x2d, conv_w, conv_b, ln_g, ln_b, w_out, b_out, gate1, nb, seq, tt=256):
    t, d = x2d.shape
    cw = conv_w.shape[1]
    aw = attn.shape[1]
    ab3 = ab.reshape(nb, seq, 2 * cw)
    attn3 = attn.reshape(nb, seq, aw)
    x3 = x2d.reshape(nb, seq, d)
    hsub = tt // CONV_HALO
    out = pl.pallas_call(
        _mix_kernel,
        out_shape=jax.ShapeDtypeStruct((nb, seq, d), F32),
        grid=(nb, seq // tt),
        in_specs=[
            pl.BlockSpec((None, tt, 2 * cw), lambda b, i: (b, i, 0)),
            pl.BlockSpec((None, CONV_HALO, 2 * cw), lambda b, i: (b, jnp.maximum(i * hsub - 1, 0), 0)),
            pl.BlockSpec((None, tt, aw), lambda b, i: (b, i, 0)),
            pl.BlockSpec((None, tt, d), lambda b, i: (b, i, 0)),
            pl.BlockSpec((CONV_SIZE, cw), lambda b, i: (0, 0)),
            pl.BlockSpec((1, cw), lambda b, i: (0, 0)),
            pl.BlockSpec((1, cw), lambda b, i: (0, 0)),
            pl.BlockSpec((1, cw), lambda b, i: (0, 0)),
            pl.BlockSpec((d, d), lambda b, i: (0, 0)),
            pl.BlockSpec((1, d), lambda b, i: (0, 0)),
            pl.BlockSpec((1, 1, d), lambda b, i: (b, 0, 0)),
        ],
        out_specs=pl.BlockSpec((None, tt, d), lambda b, i: (b, i, 0)),
        scratch_shapes=[pltpu.VMEM((SUBLANES, CONV_HALO + tt, cw), F32), pltpu.VMEM((tt, cw), F32)],
        compiler_params=_params(("arbitrary", "arbitrary")),
        name="mix",
    )(ab3, ab3, attn3, x3, conv_w, conv_b.reshape(1, cw), ln_g.reshape(1, cw), ln_b.reshape(1, cw),
      w_out.astype(BF16), b_out.reshape(1, d), gate1.reshape(nb, 1, d))
    return out.reshape(t, d)


def _split_bf16(a):
    hi = a.astype(BF16)
    lo = (a - hi.astype(F32)).astype(BF16)
    return hi, lo


PACK_SUB = SUBLANES
HI_MASK = 0xFFFF0000


def _pack_rows(v):
    half = v.shape[1] // 2
    assert half == PACK_SUB * LANES
    bits = lax.bitcast_convert_type(v.astype(BF16).astype(F32), jnp.uint32)
    word = (bits[:, :half] >> 16) | (bits[:, half:] & jnp.uint32(HI_MASK))
    planes = jnp.stack([word[:, s * LANES:(s + 1) * LANES] for s in range(PACK_SUB)], axis=0)
    return pltpu.einshape("smr->msr", planes)


def _unpack_rows(p):
    planes = pltpu.einshape("msr->smr", p)
    lo = [lax.bitcast_convert_type(planes[s] << 16, F32) for s in range(PACK_SUB)]
    hi = [lax.bitcast_convert_type(planes[s] & jnp.uint32(HI_MASK), F32) for s in range(PACK_SUB)]
    return jnp.concatenate(lo + hi, axis=1)


def _router_kernel(x_ref, g_ref, sh_ref, sc_ref, whi_ref, wlo_ref, b_ref, h_ref, r_ref):
    h = _rmsnorm_mod(x_ref[...], g_ref[...], sh_ref[0], sc_ref[0])
    h_ref[...] = _pack_rows(h)
    hi, lo = _split_bf16(h)
    logits = (jnp.dot(hi, whi_ref[...], preferred_element_type=F32)
              + jnp.dot(hi, wlo_ref[...], preferred_element_type=F32)
              + jnp.dot(lo, whi_ref[...], preferred_element_type=F32)) + b_ref[...]
    tm = logits.shape[0]
    lane = lax.broadcasted_iota(jnp.int32, (tm, LANES), 1)
    big = jnp.int32(LANES)

    def first_argmax(v):
        m = jnp.max(v, axis=-1, keepdims=True)
        idx = jnp.min(jnp.where(v == m, lane, big), axis=-1, keepdims=True)
        return m, idx

    is_group = lane < N_GROUPS
    gl = jnp.where(is_group, logits, NEG_INF)
    gmax, gsel = first_argmax(gl)
    p_g = 1.0 / jnp.sum(jnp.where(is_group, jnp.exp(gl - gmax), 0.0), axis=-1, keepdims=True)
    e_lo = N_GROUPS + gsel * EXPERTS_PER_GROUP
    in_group = (lane >= e_lo) & (lane < e_lo + EXPERTS_PER_GROUP)
    el = jnp.where(in_group, logits, NEG_INF)
    t1, i1 = first_argmax(el)
    el2 = jnp.where(lane == i1, NEG_INF, el)
    t2, i2 = first_argmax(el2)
    e2 = jnp.exp(t2 - t1)
    w1 = p_g / (1.0 + e2)
    w2 = p_g * e2 / (1.0 + e2)
    id1 = (i1 - N_GROUPS).astype(F32)
    id2 = (i2 - N_GROUPS).astype(F32)
    r_ref[...] = jnp.where(lane == 0, id1, jnp.where(lane == 1, id2,
                           jnp.where(lane == 2, w1, jnp.where(lane == 3, w2, 0.0))))


def _router(x2, g_ffn, shift2, scale2, w_gr, b_gr, w_er, b_er, seq, tm=512):
    t, d = x2.shape
    nb = shift2.shape[0]
    w = jnp.zeros((d, LANES), F32).at[:, :N_GROUPS].set(w_gr).at[:, N_GROUPS:N_GROUPS + N_EXPERTS].set(w_er)
    bias = jnp.zeros((1, LANES), F32).at[0, :N_GROUPS].set(b_gr).at[0, N_GROUPS:N_GROUPS + N_EXPERTS].set(b_er)
    whi = w.astype(BF16)
    wlo = (w - whi.astype(F32)).astype(BF16)
    steps_per_seq = seq // tm
    bvec = lambda i: (i // steps_per_seq, 0, 0)
    return pl.pallas_call(
        _router_kernel,
        out_shape=(jax.ShapeDtypeStruct((t, PACK_SUB, LANES), jnp.uint32),
                   jax.ShapeDtypeStruct((t, LANES), F32)),
        grid=(t // tm,),
        in_specs=[
            pl.BlockSpec((tm, d), lambda i: (i, 0)),
            pl.BlockSpec((1, d), lambda i: (0, 0)),
            pl.BlockSpec((1, 1, d), bvec),
            pl.BlockSpec((1, 1, d), bvec),
            pl.BlockSpec((d, LANES), lambda i: (0, 0)),
            pl.BlockSpec((d, LANES), lambda i: (0, 0)),
            pl.BlockSpec((1, LANES), lambda i: (0, 0)),
        ],
        out_specs=(pl.BlockSpec((tm, PACK_SUB, LANES), lambda i: (i, 0, 0)),
                   pl.BlockSpec((tm, LANES), lambda i: (i, 0))),
        compiler_params=_params(("arbitrary",)),
        name="router",
    )(x2, g_ffn.reshape(1, d), shift2.reshape(nb, 1, d), scale2.reshape(nb, 1, d), whi, wlo, bias)


def _final_kernel(x_ref, y0_ref, y1_ref, r_ref, gate_ref, g_ref, o_ref):
    w0 = r_ref[:, 2:3]
    w1 = r_ref[:, 3:4]
    ffn = w0 * _unpack_rows(y0_ref[...]) + w1 * _unpack_rows(y1_ref[...])
    x = x_ref[...] + gate_ref[0] * ffn
    ms = jnp.mean(x * x, axis=-1, keepdims=True)
    o_ref[...] = (x * lax.rsqrt(ms + RMS_EPS)) * g_ref[...]


def _final(x2, ys, route, gate2, g_final, seq, tm=512):
    t, d = x2.shape
    nb = gate2.shape[0]
    steps_per_seq = seq // tm
    return pl.pallas_call(
        _final_kernel,
        out_shape=jax.ShapeDtypeStruct((t, d), F32),
        grid=(t // tm,),
        in_specs=[
            pl.BlockSpec((tm, d), lambda i: (i, 0)),
            pl.BlockSpec((tm, PACK_SUB, LANES), lambda i: (i, 0, 0)),
            pl.BlockSpec((tm, PACK_SUB, LANES), lambda i: (i + t // tm, 0, 0)),
            pl.BlockSpec((tm, LANES), lambda i: (i, 0)),
            pl.BlockSpec((1, 1, d), lambda i: (i // steps_per_seq, 0, 0)),
            pl.BlockSpec((1, d), lambda i: (0, 0)),
        ],
        out_specs=pl.BlockSpec((tm, d), lambda i: (i, 0)),
        compiler_params=_params(("arbitrary",)),
        name="final",
    )(x2, ys, ys, route, gate2.reshape(nb, 1, d), g_final.reshape(1, d))


MOE_ROWS = 256
ROW_BUFS = 3
BLK_LEAD = 1
BLK_TRAIL = 2


def _moe_kernel(blk_e, blk_ord, exp_list, n_active, blk_n, blk_i0, src_tok, dst_row,
                h_hbm, wgu_hbm, wd_hbm, ys_hbm,
                h0, h1, h2, o0, o1, o2, wgu_st, wd_st, wgu_bf, wd_bf, gsem, ssem, wsem):
    b = pl.program_id(0)
    tm = h0.shape[0]
    dump0 = ys_hbm.shape[0] - ROW_BUFS * tm
    hbufs, obufs = (h0, h1, h2), (o0, o1, o2)

    def count(blk):
        return blk_n[blk + BLK_LEAD]

    def first(blk):
        return blk_i0[blk + BLK_LEAD]

    def gather_row(i0, j, s):
        return pltpu.make_async_copy(h_hbm.at[src_tok[i0 + j]], hbufs[s].at[j], gsem.at[s])

    def scatter_row(i0, n_rows, j, s):
        row = jnp.where(j < n_rows, dst_row[i0 + j], dump0 + s * tm + j)
        return pltpu.make_async_copy(obufs[s].at[j], ys_hbm.at[row], ssem.at[s])

    def wait_gather(s):
        pltpu.make_async_copy(h_hbm.at[pl.ds(0, tm)], hbufs[s], gsem.at[s]).wait()

    def wait_scatter(s):
        pltpu.make_async_copy(obufs[s], ys_hbm.at[pl.ds(0, tm)], ssem.at[s]).wait()

    def loop_rows(start_row):
        def body(j, carry):
            start_row(j)
            return carry
        lax.fori_loop(0, tm, body, 0, unroll=8)

    @pl.when(b == 0)
    def _():
        for s in range(ROW_BUFS):
            obufs[s][...] = jnp.zeros(obufs[s].shape, obufs[s].dtype)
            dump = pltpu.make_async_copy(obufs[s], ys_hbm.at[pl.ds(dump0 + s * tm, tm)], ssem.at[s])
            dump.start()
            dump.wait()
        for blk in range(ROW_BUFS - 1):
            i0 = first(blk)
            loop_rows(lambda j, i0=i0, blk=blk: gather_row(i0, j, blk).start())

    def weight_copies(ordinal, slot):
        e = exp_list[ordinal]
        return (pltpu.make_async_copy(wgu_hbm.at[e], wgu_st.at[slot], wsem.at[slot, 0]),
                pltpu.make_async_copy(wd_hbm.at[e], wd_st.at[slot], wsem.at[slot, 1]))

    @pl.when(b == 0)
    def _():
        for ordinal in range(2):
            @pl.when(ordinal < n_active[0])
            def _():
                for cp in weight_copies(ordinal, ordinal):
                    cp.start()

    @pl.when(count(b) > 0)
    def _():
        prev_e = blk_e[jnp.maximum(b - 1, 0)]

        @pl.when(jnp.logical_or(b == 0, blk_e[b] != prev_e))
        def _():
            ordinal = blk_ord[b]
            slot = ordinal % 2
            for cp in weight_copies(ordinal, slot):
                cp.wait()
            wgu_bf[...] = wgu_st[slot].astype(BF16)
            wd_bf[...] = wd_st[slot].astype(BF16)

            @pl.when(ordinal + 2 < n_active[0])
            def _():
                for cp in weight_copies(ordinal + 2, slot):
                    cp.start()

        def block_step(s):
            t = (s + ROW_BUFS - 1) % ROW_BUFS
            u = (s + 1) % ROW_BUFS
            wait_gather(s)

            @pl.when(b >= ROW_BUFS - 1)
            def _():
                wait_scatter(s)

            i0_next, i0_prev, n_prev = first(b + 2), first(b - 1), count(b - 1)
            for j in range(tm):
                gather_row(i0_next, j, t).start(priority=j % 2)
                scatter_row(i0_prev, n_prev, j, t).start(priority=j % 2)
            h = _unpack_rows(hbufs[s][...]).astype(BF16)
            gu = jnp.dot(h, wgu_bf[...], preferred_element_type=F32)
            ff = gu.shape[1] // 2
            gate, up = gu[:, :ff], gu[:, ff:]
            act = (gate * jax.nn.sigmoid(gate) * up).astype(BF16)
            obufs[s][...] = _pack_rows(jnp.dot(act, wd_bf[...], preferred_element_type=F32))

            @pl.when(count(b + 1) == 0)
            def _():
                i0, n_rows = first(b), count(b)
                loop_rows(lambda j: scatter_row(i0, n_rows, j, s).start())

                @pl.when(b >= 1)
                def _():
                    wait_scatter(u)
                wait_scatter(t)
                wait_scatter(s)
                wait_gather(u)
                wait_gather(t)

        for s in range(ROW_BUFS):
            pl.when(b % ROW_BUFS == s)(functools.partial(block_step, s))


def _moe_plan(route, tm, t):
    ids = route[:, 0:TOP_K].astype(jnp.int32)
    flat_e = ids.reshape(-1)
    n_assign = flat_e.shape[0]
    order = jnp.argsort(flat_e).astype(jnp.int32)
    counts = jnp.sum(flat_e[:, None] == jnp.arange(N_EXPERTS, dtype=jnp.int32)[None, :], axis=0,
                     dtype=jnp.int32)
    raw_start = jnp.cumsum(counts) - counts
    nblk = (counts + tm - 1) // tm
    blk_end = jnp.cumsum(nblk)
    n_blocks = -(-(n_assign + N_EXPERTS * (tm - 1)) // tm)
    bidx = jnp.arange(n_blocks, dtype=jnp.int32)
    total = blk_end[-1]
    e_of = jnp.searchsorted(blk_end, jnp.minimum(bidx, total - 1), side='right').astype(jnp.int32)
    e_of = jnp.minimum(e_of, N_EXPERTS - 1)
    j_in = bidx - (blk_end[e_of] - nblk[e_of])
    blk_i0 = raw_start[e_of] + j_in * tm
    blk_n = jnp.where(bidx < total, jnp.clip(counts[e_of] - j_in * tm, 0, tm), 0).astype(jnp.int32)
    blk_i0 = jnp.where(bidx < total, blk_i0, 0).astype(jnp.int32)
    lead, trail = jnp.zeros((BLK_LEAD,), jnp.int32), jnp.zeros((BLK_TRAIL,), jnp.int32)
    blk_n = jnp.concatenate([lead, blk_n, trail])
    blk_i0 = jnp.concatenate([lead, blk_i0, trail])
    tok, k = order // TOP_K, order % TOP_K
    spare = jnp.arange(tm, dtype=jnp.int32) % t
    src_tok = jnp.concatenate([tok, spare])
    dst_row = jnp.concatenate([k * t + tok, spare])
    active = counts > 0
    ordinal_of = (jnp.cumsum(active) - 1).astype(jnp.int32)
    blk_ord = ordinal_of[e_of]
    exp_list = jnp.argsort(jnp.logical_not(active), stable=True).astype(jnp.int32)
    n_active = jnp.sum(active, dtype=jnp.int32).reshape(1)
    return e_of, blk_ord, exp_list, n_active, blk_n, blk_i0, src_tok, dst_row, n_blocks


def _moe(h2p, route, w_gate_up, w_down, tm=MOE_ROWS):
    t = h2p.shape[0]
    d, ff2 = w_gate_up.shape[1], w_gate_up.shape[2]
    n_assign = t * TOP_K
    *tables, n_blocks = _moe_plan(route, tm, t)
    row_buf = pltpu.VMEM((tm, PACK_SUB, LANES), jnp.uint32)
    return pl.pallas_call(
        _moe_kernel,
        out_shape=jax.ShapeDtypeStruct((n_assign + ROW_BUFS * tm, PACK_SUB, LANES), jnp.uint32),
        grid_spec=pltpu.PrefetchScalarGridSpec(
            num_scalar_prefetch=len(tables),
            grid=(n_blocks,),
            in_specs=[pl.BlockSpec(memory_space=pl.ANY)] * 3,
            out_specs=pl.BlockSpec(memory_space=pl.ANY),
            scratch_shapes=[row_buf] * (2 * ROW_BUFS) + [
                            pltpu.VMEM((2, d, ff2), F32), pltpu.VMEM((2, ff2 // 2, d), F32),
                            pltpu.VMEM((d, ff2), BF16), pltpu.VMEM((ff2 // 2, d), BF16),
                            pltpu.SemaphoreType.DMA((ROW_BUFS,)), pltpu.SemaphoreType.DMA((ROW_BUFS,)),
                            pltpu.SemaphoreType.DMA((2, 2))],
        ),
        compiler_params=_params(("arbitrary",)),
        name="moe",
    )(*tables, h2p, w_gate_up, w_down)


def kernel(x, c, positions, w_ada, b_ada, g_mix, w_in, b_in, attn_sinks, conv_w, conv_b, conv_ln_g,
           conv_ln_b, w_out, b_out, g_ffn, w_group_router, b_group_router, w_expert_router,
           b_expert_router, w_gate_up, w_down, g_final):
    nb, seq, d = x.shape
    t = nb * seq
    mod = _ada(c, w_ada, b_ada)
    shift1, scale1, gate1, shift2, scale2, gate2 = [mod[:, i * d:(i + 1) * d] for i in range(6)]
    x2d = x.reshape(t, d)
    q, kv, ab = _inproj(x2d, positions.reshape(t, 1), g_mix, shift1, scale1, w_in, b_in, seq)
    attn = _attn(q, kv, attn_sinks, nb, seq)
    x2 = _mix(ab, attn, x2d, conv_w, conv_b, conv_ln_g, conv_ln_b, w_out, b_out, gate1, nb, seq)
    h2, route = _router(x2, g_ffn, shift2, scale2, w_group_router, b_group_router,
                        w_expert_router, b_expert_router, seq)
    ys = _moe(h2, route, w_gate_up, w_down)
    out = _final(x2, ys, route, gate2, g_final, seq)
    return out.reshape(nb, seq, d)
```

```python
import functools
import math

import jax
import jax.numpy as jnp
from jax import lax
from jax.experimental import pallas as pl
from jax.experimental.pallas import tpu as pltpu

F32 = jnp.float32
BF16 = jnp.bfloat16

HEAD_DIM = 64
N_Q_HEADS = 16
N_KV_HEADS = 2
ATTN_WIDTH = N_Q_HEADS * HEAD_DIM
KV_WIDTH = N_KV_HEADS * HEAD_DIM
CONV_SIZE = 31
WINDOW = 128
ROPE_THETA = 10000.0
N_GROUPS = 8
EXPERTS_PER_GROUP = 8
N_EXPERTS = N_GROUPS * EXPERTS_PER_GROUP
TOP_K = 2
EXPERT_FF = 512
RMS_EPS = 1e-6
LN_EPS = 1e-5
NEG_INF = -1e30

LANES = 128
SUBLANES = 8
VMEM_LIMIT = 56 * 1024 * 1024


def _params(semantics, vmem=VMEM_LIMIT):
    return pltpu.CompilerParams(dimension_semantics=semantics, vmem_limit_bytes=vmem)


def _ada_kernel(cb_ref, w_ref, b_ref, o_ref, s_ref):
    nb, d = cb_ref.shape[0], w_ref.shape[0]
    nl = w_ref.shape[1] // LANES
    cv = cb_ref[...]
    s_ref[...] = cv * jax.nn.sigmoid(cv)

    def body(k, acc):
        r = pl.multiple_of(k * SUBLANES, SUBLANES)
        w8 = w_ref[pl.ds(r, SUBLANES), :]
        new = []
        for b in range(nb):
            sv = s_ref[b, pl.ds(r, SUBLANES), :]
            for j in range(nl):
                new.append(acc[b * nl + j] + w8[:, j * LANES:(j + 1) * LANES] * sv)
        return tuple(new)

    init = tuple(jnp.zeros((SUBLANES, LANES), F32) for _ in range(nb * nl))
    acc = lax.fori_loop(0, d // SUBLANES, body, init, unroll=4)
    for b in range(nb):
        row = jnp.concatenate(
            [jnp.sum(acc[b * nl + j], axis=0, keepdims=True) for j in range(nl)], axis=1)
        o_ref[b:b + 1, :] = row + b_ref[...]


def _ada(c, w_ada, b_ada, tn=1024):
    nb, d = c.shape
    n = w_ada.shape[1]
    cb = jnp.broadcast_to(c[:, :, None], (nb, d, LANES))
    return pl.pallas_call(
        _ada_kernel,
        out_shape=jax.ShapeDtypeStruct((nb, n), F32),
        grid=(n // tn,),
        in_specs=[
            pl.BlockSpec((nb, d, LANES), lambda j: (0, 0, 0)),
            pl.BlockSpec((d, tn), lambda j: (0, j)),
            pl.BlockSpec((1, tn), lambda j: (0, j)),
        ],
        out_specs=pl.BlockSpec((nb, tn), lambda j: (0, j)),
        scratch_shapes=[pltpu.VMEM((nb, d, LANES), F32)],
        compiler_params=_params(("arbitrary",)),
        name="ada",
    )(cb, w_ada, b_ada.reshape(1, n))


def _rmsnorm_mod(x, g, shift, scale):
    ms = jnp.mean(x * x, axis=-1, keepdims=True)
    h = (x * lax.rsqrt(ms + RMS_EPS)) * g
    return h * (1.0 + scale) + shift


def _inproj_kernel(x_ref, pos_ref, g_ref, sh_ref, sc_ref, w_ref, b_ref, invf_ref, sgn_ref,
                   q_ref, kv_ref, ab_ref):
    h = _rmsnorm_mod(x_ref[...], g_ref[...], sh_ref[0], sc_ref[0]).astype(BF16)

    ang = pos_ref[...].astype(F32) * invf_ref[...]
    cosv = jnp.cos(ang)
    sinv = jnp.sin(ang) * sgn_ref[...]
    lane = lax.broadcasted_iota(jnp.int32, (1, LANES), 1)
    first_half = (lane % HEAD_DIM) < (HEAD_DIM // 2)

    def rope(t):
        rot = jnp.where(first_half, pltpu.roll(t, LANES - HEAD_DIM // 2, 1),
                        pltpu.roll(t, HEAD_DIM // 2, 1))
        return t * cosv + rot * sinv

    def proj(c0, width):
        return (jnp.dot(h, w_ref[:, c0:c0 + width], preferred_element_type=F32)
                + b_ref[:, c0:c0 + width])

    qscale = 1.0 / math.sqrt(HEAD_DIM)
    step = 512
    for c0 in range(0, ATTN_WIDTH, step):
        z = proj(c0, step)
        for j in range(step // LANES):
            q_ref[:, c0 + j * LANES:c0 + (j + 1) * LANES] = (
                rope(z[:, j * LANES:(j + 1) * LANES]) * qscale).astype(BF16)
    z = proj(ATTN_WIDTH, 2 * KV_WIDTH)
    k = rope(z[:, :KV_WIDTH])
    v = z[:, KV_WIDTH:]
    kv_ref[:, 0 * LANES:1 * LANES] = k.astype(BF16)
    kv_ref[:, 1 * LANES:2 * LANES] = pltpu.roll(k, HEAD_DIM, 1).astype(BF16)
    kv_ref[:, 2 * LANES:3 * LANES] = v.astype(BF16)
    kv_ref[:, 3 * LANES:4 * LANES] = pltpu.roll(v, HEAD_DIM, 1).astype(BF16)
    base = ATTN_WIDTH + 2 * KV_WIDTH
    for c0 in range(0, ab_ref.shape[1], step):
        ab_ref[:, c0:c0 + step] = proj(base + c0, step).astype(BF16)


def _inproj(x2d, pos2d, g_mix, shift1, scale1, w_in, b_in, seq, tm=512):
    t, d = x2d.shape
    n = w_in.shape[1]
    nb = shift1.shape[0]
    half = HEAD_DIM // 2
    inv_freq = ROPE_THETA ** (-jnp.arange(half, dtype=F32) * 2.0 / HEAD_DIM)
    invf = jnp.tile(inv_freq, LANES // half).reshape(1, LANES)
    sgn = jnp.tile(jnp.concatenate([-jnp.ones((half,), F32), jnp.ones((half,), F32)]),
                   LANES // HEAD_DIM).reshape(1, LANES)
    steps_per_seq = seq // tm
    conv_w2 = n - ATTN_WIDTH - 2 * KV_WIDTH
    bvec = lambda i: (i // steps_per_seq, 0, 0)
    return pl.pallas_call(
        _inproj_kernel,
        out_shape=(jax.ShapeDtypeStruct((t, ATTN_WIDTH), BF16),
                   jax.ShapeDtypeStruct((t, 4 * KV_WIDTH), BF16),
                   jax.ShapeDtypeStruct((t, conv_w2), BF16)),
        grid=(t // tm,),
        in_specs=[
            pl.BlockSpec((tm, d), lambda i: (i, 0)),
            pl.BlockSpec((tm, 1), lambda i: (i, 0)),
            pl.BlockSpec((1, d), lambda i: (0, 0)),
            pl.BlockSpec((1, 1, d), bvec),
            pl.BlockSpec((1, 1, d), bvec),
            pl.BlockSpec((d, n), lambda i: (0, 0)),
            pl.BlockSpec((1, n), lambda i: (0, 0)),
            pl.BlockSpec((1, LANES), lambda i: (0, 0)),
            pl.BlockSpec((1, LANES), lambda i: (0, 0)),
        ],
        out_specs=(pl.BlockSpec((tm, ATTN_WIDTH), lambda i: (i, 0)),
                   pl.BlockSpec((tm, 4 * KV_WIDTH), lambda i: (i, 0)),
                   pl.BlockSpec((tm, conv_w2), lambda i: (i, 0))),
        compiler_params=_params(("arbitrary",)),
        name="inproj",
    )(x2d, pos2d, g_mix.reshape(1, d), shift1.reshape(nb, 1, d), scale1.reshape(nb, 1, d),
      w_in.astype(BF16), b_in.reshape(1, n), invf, sgn)


def _attn_kernel(sink_ref, q_ref, kvm_ref, kvh_ref, o_ref):
    i = pl.program_id(1)
    tq = q_ref.shape[0]
    blk = WINDOW
    heads_per_group = N_Q_HEADS // N_KV_HEADS
    pairs = heads_per_group // 2
    row = lax.broadcasted_iota(jnp.int32, (blk, 2 * blk), 0)
    col = lax.broadcasted_iota(jnp.int32, (blk, 2 * blk), 1)
    rel = row + blk - col
    band = (rel >= 0) & (rel < WINDOW)
    lane = lax.broadcasted_iota(jnp.int32, (1, LANES), 1)
    lo = lane < HEAD_DIM
    zero = jnp.zeros((), BF16)

    for j in range(tq // blk):
        if j == 0:
            kv = jnp.concatenate([kvh_ref[...], kvm_ref[0:blk, :]], axis=0)
            valid = band & ((col >= blk) | (i > 0))
        else:
            kv = kvm_ref[(j - 1) * blk:(j + 1) * blk, :]
            valid = band
        k_nat, k_swp = kv[:, 0:LANES], kv[:, LANES:2 * LANES]
        v_nat, v_swp = kv[:, 2 * LANES:3 * LANES], kv[:, 3 * LANES:4 * LANES]
        for g in range(N_KV_HEADS):
            src_lo_k, src_hi_k = (k_nat, k_swp) if g == 0 else (k_swp, k_nat)
            src_lo_v, src_hi_v = (v_nat, v_swp) if g == 0 else (v_swp, v_nat)
            k_even = jnp.where(lo, src_lo_k, zero)
            k_odd = jnp.where(lo, zero, src_hi_k)
            v_even = jnp.where(lo, src_lo_v, zero)
            v_odd = jnp.where(lo, zero, src_hi_v)
            qs = jnp.concatenate(
                [q_ref[j * blk:(j + 1) * blk, (g * pairs + p) * LANES:(g * pairs + p + 1) * LANES]
                 for p in range(pairs)], axis=0)
            nt = (((1,), (1,)), ((), ()))
            s_even = lax.dot_general(qs, k_even, nt, preferred_element_type=F32)
            s_odd = lax.dot_general(qs, k_odd, nt, preferred_element_type=F32)
            probs, rinv = [], []
            for s_all, parity in ((s_even, 0), (s_odd, 1)):
                ps, rs = [], []
                for p in range(pairs):
                    sink = sink_ref[g * heads_per_group + 2 * p + parity]
                    s = jnp.where(valid, s_all[p * blk:(p + 1) * blk, :], NEG_INF)
                    m = jnp.maximum(jnp.max(s, axis=-1, keepdims=True), sink)
                    e = jnp.exp(s - m)
                    den = jnp.sum(e, axis=-1, keepdims=True) + jnp.exp(sink - m)
                    ps.append(e.astype(BF16))
                    rs.append(1.0 / den)
                probs.append(jnp.concatenate(ps, axis=0))
                rinv.append(rs)
            o = (jnp.dot(probs[0], v_even, preferred_element_type=F32)
                 + jnp.dot(probs[1], v_odd, preferred_element_type=F32))
            for p in range(pairs):
                scale = jnp.where(lo, rinv[0][p], rinv[1][p])
                c = (g * pairs + p) * LANES
                o_ref[j * blk:(j + 1) * blk, c:c + LANES] = (
                    o[p * blk:(p + 1) * blk, :] * scale).astype(BF16)


def _attn(q, kv, sinks, nb, seq, tq=512):
    blk = WINDOW
    q3 = q.reshape(nb, seq, ATTN_WIDTH)
    kv3 = kv.reshape(nb, seq, 4 * KV_WIDTH)
    sub = tq // blk
    out = pl.pallas_call(
        _attn_kernel,
        out_shape=jax.ShapeDtypeStruct((nb, seq, ATTN_WIDTH), BF16),
        grid=(nb, seq // tq),
        in_specs=[
            pl.BlockSpec(memory_space=pltpu.SMEM),
            pl.BlockSpec((None, tq, ATTN_WIDTH), lambda b, i: (b, i, 0)),
            pl.BlockSpec((None, tq, 4 * KV_WIDTH), lambda b, i: (b, i, 0)),
            pl.BlockSpec((None, blk, 4 * KV_WIDTH), lambda b, i: (b, jnp.maximum(i * sub - 1, 0), 0)),
        ],
        out_specs=pl.BlockSpec((None, tq, ATTN_WIDTH), lambda b, i: (b, i, 0)),
        compiler_params=_params(("arbitrary", "arbitrary")),
        name="attn",
    )(sinks.astype(F32), q3, kv3, kv3)
    return out.reshape(nb * seq, ATTN_WIDTH)


CONV_HALO = 32
CONV_ROWS = 256


def _mix_kernel(ab_ref, abh_ref, attn_ref, x_ref, wc_ref, cb_ref, lg_ref, lb_ref, wo_ref, bo_ref,
                gate_ref, o_ref, us_ref, y_ref):
    i = pl.program_id(1)
    tt = x_ref.shape[0]
    cw = y_ref.shape[1]

    def glu(ab):
        a = ab[:, :cw].astype(F32)
        b = ab[:, cw:].astype(F32)
        return a * jax.nn.sigmoid(b)

    halo = glu(abh_ref[...])
    us_ref[0, 0:CONV_HALO, :] = jnp.where(i > 0, halo, 0.0)
    us_ref[0, CONV_HALO:, :] = glu(ab_ref[...])
    n = tt + CONV_HALO - SUBLANES
    for s in range(1, SUBLANES):
        us_ref[s, 0:n, :] = us_ref[0, s:s + n, :]

    off = CONV_HALO - (CONV_SIZE - 1)

    def conv_lanes(c, carry):
        l0 = pl.multiple_of(c * LANES, LANES)
        for r0 in range(0, tt, CONV_ROWS):
            acc = jnp.zeros((CONV_ROWS, LANES), F32)
            for k in range(CONV_SIZE):
                o = off + k
                a0 = (o // SUBLANES) * SUBLANES + r0
                acc = acc + (us_ref[o % SUBLANES, a0:a0 + CONV_ROWS, pl.ds(l0, LANES)]
                             * wc_ref[k:k + 1, pl.ds(l0, LANES)])
            y_ref[r0:r0 + CONV_ROWS, pl.ds(l0, LANES)] = acc
        return carry

    lax.fori_loop(0, cw // LANES, conv_lanes, 0)

    y = y_ref[...] + cb_ref[...]
    mu = jnp.mean(y, axis=-1, keepdims=True)
    yc = y - mu
    var = jnp.mean(yc * yc, axis=-1, keepdims=True)
    yn = yc * lax.rsqrt(var + LN_EPS) * lg_ref[...] + lb_ref[...]
    conv = (yn * jax.nn.sigmoid(yn)).astype(BF16)

    aw = attn_ref.shape[1]
    mixed = (jnp.dot(attn_ref[...], wo_ref[0:aw, :], preferred_element_type=F32)
             + jnp.dot(conv, wo_ref[aw:, :], preferred_element_type=F32) + bo_ref[...])
    o_ref[...] = x_ref[...] + gate_ref[0] * mixed


def _mix(ab, attn, x2d, conv_w, conv_b, ln_g, ln_b, w_out, b_out, gate1, nb, seq, tt=512):
    t, d = x2d.shape
    cw = conv_w.shape[1]
    aw = attn.shape[1]
    ab3 = ab.reshape(nb, seq, 2 * cw)
    attn3 = attn.reshape(nb, seq, aw)
    x3 = x2d.reshape(nb, seq, d)
    hsub = tt // CONV_HALO
    out = pl.pallas_call(
        _mix_kernel,
        out_shape=jax.ShapeDtypeStruct((nb, seq, d), F32),
        grid=(nb, seq // tt),
        in_specs=[
            pl.BlockSpec((None, tt, 2 * cw), lambda b, i: (b, i, 0)),
            pl.BlockSpec((None, CONV_HALO, 2 * cw), lambda b, i: (b, jnp.maximum(i * hsub - 1, 0), 0)),
            pl.BlockSpec((None, tt, aw), lambda b, i: (b, i, 0)),
            pl.BlockSpec((None, tt, d), lambda b, i: (b, i, 0)),
            pl.BlockSpec((CONV_SIZE, cw), lambda b, i: (0, 0)),
            pl.BlockSpec((1, cw), lambda b, i: (0, 0)),
            pl.BlockSpec((1, cw), lambda b, i: (0, 0)),
            pl.BlockSpec((1, cw), lambda b, i: (0, 0)),
            pl.BlockSpec((d, d), lambda b, i: (0, 0), pipeline_mode=pl.Buffered(1)),
            pl.BlockSpec((1, d), lambda b, i: (0, 0)),
            pl.BlockSpec((1, 1, d), lambda b, i: (b, 0, 0)),
        ],
        out_specs=pl.BlockSpec((None, tt, d), lambda b, i: (b, i, 0)),
        scratch_shapes=[pltpu.VMEM((SUBLANES, CONV_HALO + tt, cw), F32), pltpu.VMEM((tt, cw), F32)],
        compiler_params=_params(("arbitrary", "arbitrary")),
        name="mix",
    )(ab3, ab3, attn3, x3, conv_w, conv_b.reshape(1, cw), ln_g.reshape(1, cw), ln_b.reshape(1, cw),
      w_out.astype(BF16), b_out.reshape(1, d), gate1.reshape(nb, 1, d))
    return out.reshape(t, d)


PACK_SUB = SUBLANES
HI_MASK = 0xFFFF0000


def _pack_rows(v):
    return _pack_rounded(v.astype(BF16).astype(F32))


def _pack_rounded(v):
    half = v.shape[1] // 2
    assert half == PACK_SUB * LANES
    bits = lax.bitcast_convert_type(v, jnp.uint32)
    word = (bits[:, :half] >> 16) | (bits[:, half:] & jnp.uint32(HI_MASK))
    planes = jnp.stack([word[:, s * LANES:(s + 1) * LANES] for s in range(PACK_SUB)], axis=0)
    return pltpu.einshape("smr->msr", planes)


def _unpack_rows(p):
    planes = pltpu.einshape("msr->smr", p)
    lo = [lax.bitcast_convert_type(planes[s] << 16, F32) for s in range(PACK_SUB)]
    hi = [lax.bitcast_convert_type(planes[s] & jnp.uint32(HI_MASK), F32) for s in range(PACK_SUB)]
    return jnp.concatenate(lo + hi, axis=1)


def _router_kernel(x_ref, g_ref, sh_ref, sc_ref, whi_ref, wlo_ref, b_ref, h_ref, r_ref):
    h = _rmsnorm_mod(x_ref[...], g_ref[...], sh_ref[0], sc_ref[0])
    hi = h.astype(BF16)
    hi_f32 = hi.astype(F32)
    h_ref[...] = _pack_rounded(hi_f32)
    lo = (h - hi_f32).astype(BF16)
    logits = (jnp.dot(hi, whi_ref[...], preferred_element_type=F32)
              + jnp.dot(hi, wlo_ref[...], preferred_element_type=F32)
              + jnp.dot(lo, whi_ref[...], preferred_element_type=F32)) + b_ref[...]
    tm = logits.shape[0]
    lane = lax.broadcasted_iota(jnp.int32, (tm, LANES), 1)
    big = jnp.int32(LANES)

    def first_argmax(v):
        m = jnp.max(v, axis=-1, keepdims=True)
        idx = jnp.min(jnp.where(v == m, lane, big), axis=-1, keepdims=True)
        return m, idx

    is_group = lane < N_GROUPS
    gl = jnp.where(is_group, logits, NEG_INF)
    gmax, gsel = first_argmax(gl)
    p_g = 1.0 / jnp.sum(jnp.where(is_group, jnp.exp(gl - gmax), 0.0), axis=-1, keepdims=True)
    e_lo = N_GROUPS + gsel * EXPERTS_PER_GROUP
    in_group = (lane >= e_lo) & (lane < e_lo + EXPERTS_PER_GROUP)
    el = jnp.where(in_group, logits, NEG_INF)
    t1, i1 = first_argmax(el)
    el2 = jnp.where(lane == i1, NEG_INF, el)
    t2, i2 = first_argmax(el2)
    e2 = jnp.exp(t2 - t1)
    w1 = p_g / (1.0 + e2)
    w2 = p_g * e2 / (1.0 + e2)
    id1 = (i1 - N_GROUPS).astype(F32)
    id2 = (i2 - N_GROUPS).astype(F32)
    r_ref[...] = jnp.where(lane == 0, id1, jnp.where(lane == 1, id2,
                           jnp.where(lane == 2, w1, jnp.where(lane == 3, w2, 0.0))))


def _router(x2, g_ffn, shift2, scale2, w_gr, b_gr, w_er, b_er, seq, tm=512):
    t, d = x2.shape
    nb = shift2.shape[0]
    w = jnp.zeros((d, LANES), F32).at[:, :N_GROUPS].set(w_gr).at[:, N_GROUPS:N_GROUPS + N_EXPERTS].set(w_er)
    bias = jnp.zeros((1, LANES), F32).at[0, :N_GROUPS].set(b_gr).at[0, N_GROUPS:N_GROUPS + N_EXPERTS].set(b_er)
    whi = w.astype(BF16)
    wlo = (w - whi.astype(F32)).astype(BF16)
    steps_per_seq = seq // tm
    bvec = lambda i: (i // steps_per_seq, 0, 0)
    return pl.pallas_call(
        _router_kernel,
        out_shape=(jax.ShapeDtypeStruct((t, PACK_SUB, LANES), jnp.uint32),
                   jax.ShapeDtypeStruct((t, LANES), F32)),
        grid=(t // tm,),
        in_specs=[
            pl.BlockSpec((tm, d), lambda i: (i, 0)),
            pl.BlockSpec((1, d), lambda i: (0, 0)),
            pl.BlockSpec((1, 1, d), bvec),
            pl.BlockSpec((1, 1, d), bvec),
            pl.BlockSpec((d, LANES), lambda i: (0, 0)),
            pl.BlockSpec((d, LANES), lambda i: (0, 0)),
            pl.BlockSpec((1, LANES), lambda i: (0, 0)),
        ],
        out_specs=(pl.BlockSpec((tm, PACK_SUB, LANES), lambda i: (i, 0, 0)),
                   pl.BlockSpec((tm, LANES), lambda i: (i, 0))),
        compiler_params=_params(("arbitrary",)),
        name="router",
    )(x2, g_ffn.reshape(1, d), shift2.reshape(nb, 1, d), scale2.reshape(nb, 1, d), whi, wlo, bias)


def _final_kernel(x_ref, y0_ref, y1_ref, r_ref, gate_ref, g_ref, o_ref):
    w0 = r_ref[:, 2:3]
    w1 = r_ref[:, 3:4]
    ffn = w0 * _unpack_rows(y0_ref[...]) + w1 * _unpack_rows(y1_ref[...])
    x = x_ref[...] + gate_ref[0] * ffn
    ms = jnp.mean(x * x, axis=-1, keepdims=True)
    o_ref[...] = (x * lax.rsqrt(ms + RMS_EPS)) * g_ref[...]


def _final(x2, ys, route, gate2, g_final, seq, tm=1024):
    t, d = x2.shape
    nb = gate2.shape[0]
    steps_per_seq = seq // tm
    return pl.pallas_call(
        _final_kernel,
        out_shape=jax.ShapeDtypeStruct((t, d), F32),
        grid=(t // tm,),
        in_specs=[
            pl.BlockSpec((tm, d), lambda i: (i, 0)),
            pl.BlockSpec((tm, PACK_SUB, LANES), lambda i: (i, 0, 0)),
            pl.BlockSpec((tm, PACK_SUB, LANES), lambda i: (i + t // tm, 0, 0)),
            pl.BlockSpec((tm, LANES), lambda i: (i, 0)),
            pl.BlockSpec((1, 1, d), lambda i: (i // steps_per_seq, 0, 0)),
            pl.BlockSpec((1, d), lambda i: (0, 0)),
        ],
        out_specs=pl.BlockSpec((tm, d), lambda i: (i, 0)),
        compiler_params=_params(("arbitrary",)),
        name="final",
    )(x2, ys, ys, route, gate2.reshape(nb, 1, d), g_final.reshape(1, d))


MOE_ROWS = 256
ROW_BUFS = 3
BLK_LEAD = 1
BLK_TRAIL = 2


def _moe_kernel(blk_e, blk_ord, exp_list, n_active, blk_n, blk_i0, src_tok, dst_row,
                h_hbm, wgu_hbm, wd_hbm, ys_hbm,
                h0, h1, h2, o0, o1, o2, wgu_st, wd_st, wgu_bf, wd_bf, gsem, ssem, wsem):
    b = pl.program_id(0)
    tm = h0.shape[0]
    dump0 = ys_hbm.shape[0] - ROW_BUFS * tm
    hbufs, obufs = (h0, h1, h2), (o0, o1, o2)

    def count(blk):
        return blk_n[blk + BLK_LEAD]

    def first(blk):
        return blk_i0[blk + BLK_LEAD]

    def gather_row(i0, j, s):
        return pltpu.make_async_copy(h_hbm.at[src_tok[i0 + j]], hbufs[s].at[j], gsem.at[s])

    def scatter_row(i0, n_rows, j, s):
        row = jnp.where(j < n_rows, dst_row[i0 + j], dump0 + s * tm + j)
        return pltpu.make_async_copy(obufs[s].at[j], ys_hbm.at[row], ssem.at[s])

    def wait_gather(s):
        pltpu.make_async_copy(h_hbm.at[pl.ds(0, tm)], hbufs[s], gsem.at[s]).wait()

    def wait_scatter(s):
        pltpu.make_async_copy(obufs[s], ys_hbm.at[pl.ds(0, tm)], ssem.at[s]).wait()

    def loop_rows(start_row):
        def body(j, carry):
            start_row(j)
            return carry
        lax.fori_loop(0, tm, body, 0, unroll=8)

    @pl.when(b == 0)
    def _():
        for s in range(ROW_BUFS):
            obufs[s][...] = jnp.zeros(obufs[s].shape, obufs[s].dtype)
            dump = pltpu.make_async_copy(obufs[s], ys_hbm.at[pl.ds(dump0 + s * tm, tm)], ssem.at[s])
            dump.start()
            dump.wait()
        for blk in range(ROW_BUFS - 1):
            i0 = first(blk)
            loop_rows(lambda j, i0=i0, blk=blk: gather_row(i0, j, blk).start())

    def weight_copies(ordinal, slot):
        e = exp_list[ordinal]
        return (pltpu.make_async_copy(wgu_hbm.at[e], wgu_st.at[slot], wsem.at[slot, 0]),
                pltpu.make_async_copy(wd_hbm.at[e], wd_st.at[slot], wsem.at[slot, 1]))

    @pl.when(b == 0)
    def _():
        for ordinal in range(2):
            @pl.when(ordinal < n_active[0])
            def _():
                for cp in weight_copies(ordinal, ordinal):
                    cp.start()

    @pl.when(count(b) > 0)
    def _():
        prev_e = blk_e[jnp.maximum(b - 1, 0)]

        @pl.when(jnp.logical_or(b == 0, blk_e[b] != prev_e))
        def _():
            ordinal = blk_ord[b]
            slot = ordinal % 2
            for cp in weight_copies(ordinal, slot):
                cp.wait()
            wgu_bf[...] = wgu_st[slot].astype(BF16)
            wd_bf[...] = wd_st[slot].astype(BF16)

            @pl.when(ordinal + 2 < n_active[0])
            def _():
                for cp in weight_copies(ordinal + 2, slot):
                    cp.start()

        def block_step(s):
            t = (s + ROW_BUFS - 1) % ROW_BUFS
            u = (s + 1) % ROW_BUFS
            wait_gather(s)

            @pl.when(b >= ROW_BUFS - 1)
            def _():
                wait_scatter(s)

            i0_next, i0_prev, n_prev = first(b + 2), first(b - 1), count(b - 1)
            for j in range(tm):
                gather_row(i0_next, j, t).start(priority=j % 2)
                scatter_row(i0_prev, n_prev, j, t).start(priority=j % 2)
            h = _unpack_rows(hbufs[s][...]).astype(BF16)
            gu = jnp.dot(h, wgu_bf[...], preferred_element_type=F32)
            ff = gu.shape[1] // 2
            gate, up = gu[:, :ff], gu[:, ff:]
            act = (gate * jax.nn.sigmoid(gate) * up).astype(BF16)
            obufs[s][...] = _pack_rows(jnp.dot(act, wd_bf[...], preferred_element_type=F32))

            @pl.when(count(b + 1) == 0)
            def _():
                i0, n_rows = first(b), count(b)
                loop_rows(lambda j: scatter_row(i0, n_rows, j, s).start())

                @pl.when(b >= 1)
                def _():
                    wait_scatter(u)
                wait_scatter(t)
                wait_scatter(s)
                wait_gather(u)
                wait_gather(t)

        for s in range(ROW_BUFS):
            pl.when(b % ROW_BUFS == s)(functools.partial(block_step, s))


def _moe_plan(route, tm, t):
    ids = route[:, 0:TOP_K].astype(jnp.int32)
    flat_e = ids.reshape(-1)
    n_assign = flat_e.shape[0]
    order = jnp.argsort(flat_e).astype(jnp.int32)
    counts = jnp.sum(flat_e[:, None] == jnp.arange(N_EXPERTS, dtype=jnp.int32)[None, :], axis=0,
                     dtype=jnp.int32)
    raw_start = jnp.cumsum(counts) - counts
    nblk = (counts + tm - 1) // tm
    blk_end = jnp.cumsum(nblk)
    n_blocks = -(-(n_assign + N_EXPERTS * (tm - 1)) // tm)
    bidx = jnp.arange(n_blocks, dtype=jnp.int32)
    total = blk_end[-1]
    e_of = jnp.sum(jnp.minimum(bidx, total - 1)[:, None] >= blk_end[None, :], axis=1, dtype=jnp.int32)
    e_of = jnp.minimum(e_of, N_EXPERTS - 1)
    j_in = bidx - (blk_end[e_of] - nblk[e_of])
    blk_i0 = raw_start[e_of] + j_in * tm
    blk_n = jnp.where(bidx < total, jnp.clip(counts[e_of] - j_in * tm, 0, tm), 0).astype(jnp.int32)
    blk_i0 = jnp.where(bidx < total, blk_i0, 0).astype(jnp.int32)
    lead, trail = jnp.zeros((BLK_LEAD,), jnp.int32), jnp.zeros((BLK_TRAIL,), jnp.int32)
    blk_n = jnp.concatenate([lead, blk_n, trail])
    blk_i0 = jnp.concatenate([lead, blk_i0, trail])
    tok, k = order // TOP_K, order % TOP_K
    spare = jnp.arange(tm, dtype=jnp.int32) % t
    src_tok = jnp.concatenate([tok, spare])
    dst_row = jnp.concatenate([k * t + tok, spare])
    active = counts > 0
    ordinal_of = (jnp.cumsum(active) - 1).astype(jnp.int32)
    blk_ord = ordinal_of[e_of]
    exp_list = jnp.argsort(jnp.logical_not(active), stable=True).astype(jnp.int32)
    n_active = jnp.sum(active, dtype=jnp.int32).reshape(1)
    return e_of, blk_ord, exp_list, n_active, blk_n, blk_i0, src_tok, dst_row, n_blocks


def _moe(h2p, route, w_gate_up, w_down, tm=MOE_ROWS):
    t = h2p.shape[0]
    d, ff2 = w_gate_up.shape[1], w_gate_up.shape[2]
    n_assign = t * TOP_K
    *tables, n_blocks = _moe_plan(route, tm, t)
    row_buf = pltpu.VMEM((tm, PACK_SUB, LANES), jnp.uint32)
    return pl.pallas_call(
        _moe_kernel,
        out_shape=jax.ShapeDtypeStruct((n_assign + ROW_BUFS * tm, PACK_SUB, LANES), jnp.uint32),
        grid_spec=pltpu.PrefetchScalarGridSpec(
            num_scalar_prefetch=len(tables),
            grid=(n_blocks,),
            in_specs=[pl.BlockSpec(memory_space=pl.ANY)] * 3,
            out_specs=pl.BlockSpec(memory_space=pl.ANY),
            scratch_shapes=[row_buf] * (2 * ROW_BUFS) + [
                            pltpu.VMEM((2, d, ff2), F32), pltpu.VMEM((2, ff2 // 2, d), F32),
                            pltpu.VMEM((d, ff2), BF16), pltpu.VMEM((ff2 // 2, d), BF16),
                            pltpu.SemaphoreType.DMA((ROW_BUFS,)), pltpu.SemaphoreType.DMA((ROW_BUFS,)),
                            pltpu.SemaphoreType.DMA((2, 2))],
        ),
        compiler_params=_params(("arbitrary",)),
        name="moe",
    )(*tables, h2p, w_gate_up, w_down)


def kernel(x, c, positions, w_ada, b_ada, g_mix, w_in, b_in, attn_sinks, conv_w, conv_b, conv_ln_g,
           conv_ln_b, w_out, b_out, g_ffn, w_group_router, b_group_router, w_expert_router,
           b_expert_router, w_gate_up, w_down, g_final):
    nb, seq, d = x.shape
    t = nb * seq
    mod = _ada(c, w_ada, b_ada)
    shift1, scale1, gate1, shift2, scale2, gate2 = [mod[:, i * d:(i + 1) * d] for i in range(6)]
    x2d = x.reshape(t, d)
    q, kv, ab = _inproj(x2d, positions.reshape(t, 1), g_mix, shift1, scale1, w_in, b_in, seq)
    attn = _attn(q, kv, attn_sinks, nb, seq)
    x2 = _mix(ab, attn, x2d, conv_w, conv_b, conv_ln_g, conv_ln_b, w_out, b_out, gate1, nb, seq)
    h2, route = _router(x2, g_ffn, shift2, scale2, w_group_router, b_group_router,
                        w_expert_router, b_expert_router, seq)
    ys = _moe(h2, route, w_gate_up, w_down)
    out = _final(x2, ys, route, gate2, g_final, seq)
    return out.reshape(nb, seq, d)
```

```python
import functools
import math

import jax
import jax.numpy as jnp
from jax import lax
from jax.experimental import pallas as pl
from jax.experimental.pallas import tpu as pltpu

F32 = jnp.float32
BF16 = jnp.bfloat16

HEAD_DIM = 64
N_Q_HEADS = 16
N_KV_HEADS = 2
ATTN_WIDTH = N_Q_HEADS * HEAD_DIM
KV_WIDTH = N_KV_HEADS * HEAD_DIM
CONV_SIZE = 31
WINDOW = 128
ROPE_THETA = 10000.0
N_GROUPS = 8
EXPERTS_PER_GROUP = 8
N_EXPERTS = N_GROUPS * EXPERTS_PER_GROUP
TOP_K = 2
EXPERT_FF = 512
RMS_EPS = 1e-6
LN_EPS = 1e-5
NEG_INF = -1e30

LANES = 128
SUBLANES = 8
VMEM_LIMIT = 56 * 1024 * 1024


def _params(semantics, vmem=VMEM_LIMIT):
    return pltpu.CompilerParams(dimension_semantics=semantics, vmem_limit_bytes=vmem)


def _ada_kernel(cb_ref, w_ref, b_ref, o_ref, s_ref):
    nb, d = cb_ref.shape[0], w_ref.shape[0]
    nl = w_ref.shape[1] // LANES
    cv = cb_ref[...]
    s_ref[...] = cv * jax.nn.sigmoid(cv)

    def body(k, acc):
        r = pl.multiple_of(k * SUBLANES, SUBLANES)
        w8 = w_ref[pl.ds(r, SUBLANES), :]
        new = []
        for b in range(nb):
            sv = s_ref[b, pl.ds(r, SUBLANES), :]
            for j in range(nl):
                new.append(acc[b * nl + j] + w8[:, j * LANES:(j + 1) * LANES] * sv)
        return tuple(new)

    init = tuple(jnp.zeros((SUBLANES, LANES), F32) for _ in range(nb * nl))
    acc = lax.fori_loop(0, d // SUBLANES, body, init, unroll=4)
    for b in range(nb):
        row = jnp.concatenate(
            [jnp.sum(acc[b * nl + j], axis=0, keepdims=True) for j in range(nl)], axis=1)
        o_ref[b:b + 1, :] = row + b_ref[...]


def _ada(c, w_ada, b_ada, tn=1024):
    nb, d = c.shape
    n = w_ada.shape[1]
    cb = jnp.broadcast_to(c[:, :, None], (nb, d, LANES))
    return pl.pallas_call(
        _ada_kernel,
        out_shape=jax.ShapeDtypeStruct((nb, n), F32),
        grid=(n // tn,),
        in_specs=[
            pl.BlockSpec((nb, d, LANES), lambda j: (0, 0, 0)),
            pl.BlockSpec((d, tn), lambda j: (0, j)),
            pl.BlockSpec((1, tn), lambda j: (0, j)),
        ],
        out_specs=pl.BlockSpec((nb, tn), lambda j: (0, j)),
        scratch_shapes=[pltpu.VMEM((nb, d, LANES), F32)],
        compiler_params=_params(("arbitrary",)),
        name="ada",
    )(cb, w_ada, b_ada.reshape(1, n))


def _rmsnorm_mod(x, g, shift, scale):
    ms = jnp.mean(x * x, axis=-1, keepdims=True)
    h = (x * lax.rsqrt(ms + RMS_EPS)) * g
    return h * (1.0 + scale) + shift


def _inproj_kernel(x_ref, pos_ref, g_ref, sh_ref, sc_ref, w_ref, b_ref, invf_ref, sgn_ref,
                   q_ref, kv_ref, ab_ref):
    h = _rmsnorm_mod(x_ref[...], g_ref[...], sh_ref[0], sc_ref[0]).astype(BF16)

    ang = pos_ref[...].astype(F32) * invf_ref[...]
    cosv = jnp.cos(ang)
    sinv = jnp.sin(ang) * sgn_ref[...]
    lane = lax.broadcasted_iota(jnp.int32, (1, LANES), 1)
    first_half = (lane % HEAD_DIM) < (HEAD_DIM // 2)

    def rope(t):
        rot = jnp.where(first_half, pltpu.roll(t, LANES - HEAD_DIM // 2, 1),
                        pltpu.roll(t, HEAD_DIM // 2, 1))
        return t * cosv + rot * sinv

    def proj(c0, width):
        return (jnp.dot(h, w_ref[:, c0:c0 + width], preferred_element_type=F32)
                + b_ref[:, c0:c0 + width])

    qscale = 1.0 / math.sqrt(HEAD_DIM)
    step = 512
    for c0 in range(0, ATTN_WIDTH, step):
        z = proj(c0, step)
        for j in range(step // LANES):
            q_ref[:, c0 + j * LANES:c0 + (j + 1) * LANES] = (
                rope(z[:, j * LANES:(j + 1) * LANES]) * qscale).astype(BF16)
    z = proj(ATTN_WIDTH, 2 * KV_WIDTH)
    k = rope(z[:, :KV_WIDTH])
    v = z[:, KV_WIDTH:]
    kv_ref[:, 0 * LANES:1 * LANES] = k.astype(BF16)
    kv_ref[:, 1 * LANES:2 * LANES] = pltpu.roll(k, HEAD_DIM, 1).astype(BF16)
    kv_ref[:, 2 * LANES:3 * LANES] = v.astype(BF16)
    kv_ref[:, 3 * LANES:4 * LANES] = pltpu.roll(v, HEAD_DIM, 1).astype(BF16)
    base = ATTN_WIDTH + 2 * KV_WIDTH
    for c0 in range(0, ab_ref.shape[1], step):
        ab_ref[:, c0:c0 + step] = proj(base + c0, step).astype(BF16)


def _inproj(x2d, pos2d, g_mix, shift1, scale1, w_in, b_in, seq, tm=512):
    t, d = x2d.shape
    n = w_in.shape[1]
    nb = shift1.shape[0]
    half = HEAD_DIM // 2
    inv_freq = ROPE_THETA ** (-jnp.arange(half, dtype=F32) * 2.0 / HEAD_DIM)
    invf = jnp.tile(inv_freq, LANES // half).reshape(1, LANES)
    sgn = jnp.tile(jnp.concatenate([-jnp.ones((half,), F32), jnp.ones((half,), F32)]),
                   LANES // HEAD_DIM).reshape(1, LANES)
    steps_per_seq = seq // tm
    conv_w2 = n - ATTN_WIDTH - 2 * KV_WIDTH
    bvec = lambda i: (i // steps_per_seq, 0, 0)
    return pl.pallas_call(
        _inproj_kernel,
        out_shape=(jax.ShapeDtypeStruct((t, ATTN_WIDTH), BF16),
                   jax.ShapeDtypeStruct((t, 4 * KV_WIDTH), BF16),
                   jax.ShapeDtypeStruct((t, conv_w2), BF16)),
        grid=(t // tm,),
        in_specs=[
            pl.BlockSpec((tm, d), lambda i: (i, 0)),
            pl.BlockSpec((tm, 1), lambda i: (i, 0)),
            pl.BlockSpec((1, d), lambda i: (0, 0)),
            pl.BlockSpec((1, 1, d), bvec),
            pl.BlockSpec((1, 1, d), bvec),
            pl.BlockSpec((d, n), lambda i: (0, 0)),
            pl.BlockSpec((1, n), lambda i: (0, 0)),
            pl.BlockSpec((1, LANES), lambda i: (0, 0)),
            pl.BlockSpec((1, LANES), lambda i: (0, 0)),
        ],
        out_specs=(pl.BlockSpec((tm, ATTN_WIDTH), lambda i: (i, 0)),
                   pl.BlockSpec((tm, 4 * KV_WIDTH), lambda i: (i, 0)),
                   pl.BlockSpec((tm, conv_w2), lambda i: (i, 0))),
        compiler_params=_params(("arbitrary",)),
        name="inproj",
    )(x2d, pos2d, g_mix.reshape(1, d), shift1.reshape(nb, 1, d), scale1.reshape(nb, 1, d),
      w_in.astype(BF16), b_in.reshape(1, n), invf, sgn)


def _attn_kernel(sink_ref, q_ref, kvm_ref, kvh_ref, o_ref):
    i = pl.program_id(1)
    tq = q_ref.shape[0]
    blk = WINDOW
    heads_per_group = N_Q_HEADS // N_KV_HEADS
    pairs = heads_per_group // 2
    row = lax.broadcasted_iota(jnp.int32, (blk, 2 * blk), 0)
    col = lax.broadcasted_iota(jnp.int32, (blk, 2 * blk), 1)
    rel = row + blk - col
    band = (rel >= 0) & (rel < WINDOW)
    lane = lax.broadcasted_iota(jnp.int32, (1, LANES), 1)
    lo = lane < HEAD_DIM
    zero = jnp.zeros((), BF16)

    for j in range(tq // blk):
        if j == 0:
            kv = jnp.concatenate([kvh_ref[...], kvm_ref[0:blk, :]], axis=0)
            valid = band & ((col >= blk) | (i > 0))
        else:
            kv = kvm_ref[(j - 1) * blk:(j + 1) * blk, :]
            valid = band
        k_nat, k_swp = kv[:, 0:LANES], kv[:, LANES:2 * LANES]
        v_nat, v_swp = kv[:, 2 * LANES:3 * LANES], kv[:, 3 * LANES:4 * LANES]
        for g in range(N_KV_HEADS):
            src_lo_k, src_hi_k = (k_nat, k_swp) if g == 0 else (k_swp, k_nat)
            src_lo_v, src_hi_v = (v_nat, v_swp) if g == 0 else (v_swp, v_nat)
            k_even = jnp.where(lo, src_lo_k, zero)
            k_odd = jnp.where(lo, zero, src_hi_k)
            v_even = jnp.where(lo, src_lo_v, zero)
            v_odd = jnp.where(lo, zero, src_hi_v)
            qs = jnp.concatenate(
                [q_ref[j * blk:(j + 1) * blk, (g * pairs + p) * LANES:(g * pairs + p + 1) * LANES]
                 for p in range(pairs)], axis=0)
            nt = (((1,), (1,)), ((), ()))
            s_even = lax.dot_general(qs, k_even, nt, preferred_element_type=F32)
            s_odd = lax.dot_general(qs, k_odd, nt, preferred_element_type=F32)
            probs, rinv = [], []
            for s_all, parity in ((s_even, 0), (s_odd, 1)):
                ps, rs = [], []
                for p in range(pairs):
                    sink = sink_ref[g * heads_per_group + 2 * p + parity]
                    s = jnp.where(valid, s_all[p * blk:(p + 1) * blk, :], NEG_INF)
                    m = jnp.maximum(jnp.max(s, axis=-1, keepdims=True), sink)
                    e = jnp.exp(s - m)
                    den = jnp.sum(e, axis=-1, keepdims=True) + jnp.exp(sink - m)
                    ps.append(e.astype(BF16))
                    rs.append(1.0 / den)
                probs.append(jnp.concatenate(ps, axis=0))
                rinv.append(rs)
            o = (jnp.dot(probs[0], v_even, preferred_element_type=F32)
                 + jnp.dot(probs[1], v_odd, preferred_element_type=F32))
            for p in range(pairs):
                scale = jnp.where(lo, rinv[0][p], rinv[1][p])
                c = (g * pairs + p) * LANES
                o_ref[j * blk:(j + 1) * blk, c:c + LANES] = (
                    o[p * blk:(p + 1) * blk, :] * scale).astype(BF16)


def _attn(q, kv, sinks, nb, seq, tq=512):
    blk = WINDOW
    q3 = q.reshape(nb, seq, ATTN_WIDTH)
    kv3 = kv.reshape(nb, seq, 4 * KV_WIDTH)
    sub = tq // blk
    out = pl.pallas_call(
        _attn_kernel,
        out_shape=jax.ShapeDtypeStruct((nb, seq, ATTN_WIDTH), BF16),
        grid=(nb, seq // tq),
        in_specs=[
            pl.BlockSpec(memory_space=pltpu.SMEM),
            pl.BlockSpec((None, tq, ATTN_WIDTH), lambda b, i: (b, i, 0)),
            pl.BlockSpec((None, tq, 4 * KV_WIDTH), lambda b, i: (b, i, 0)),
            pl.BlockSpec((None, blk, 4 * KV_WIDTH), lambda b, i: (b, jnp.maximum(i * sub - 1, 0), 0)),
        ],
        out_specs=pl.BlockSpec((None, tq, ATTN_WIDTH), lambda b, i: (b, i, 0)),
        compiler_params=_params(("arbitrary", "arbitrary")),
        name="attn",
    )(sinks.astype(F32), q3, kv3, kv3)
    return out.reshape(nb * seq, ATTN_WIDTH)


CONV_HALO = 32
CONV_ROWS = 256


def _mix_kernel(ab_ref, abh_ref, attn_ref, x_ref, wc_ref, cb_ref, lg_ref, lb_ref, wo_ref, bo_ref,
                gate_ref, o_ref, us_ref, y_ref):
    i = pl.program_id(1)
    tt = x_ref.shape[0]
    cw = y_ref.shape[1]

    def glu(ab):
        a = ab[:, :cw].astype(F32)
        b = ab[:, cw:].astype(F32)
        return a * jax.nn.sigmoid(b)

    halo = glu(abh_ref[...])
    us_ref[0, 0:CONV_HALO, :] = jnp.where(i > 0, halo, 0.0)
    us_ref[0, CONV_HALO:, :] = glu(ab_ref[...])
    n = tt + CONV_HALO - SUBLANES
    for s in range(1, SUBLANES):
        us_ref[s, 0:n, :] = us_ref[0, s:s + n, :]

    off = CONV_HALO - (CONV_SIZE - 1)

    def conv_lanes(c, carry):
        l0 = pl.multiple_of(c * LANES, LANES)
        for r0 in range(0, tt, CONV_ROWS):
            acc = jnp.zeros((CONV_ROWS, LANES), F32)
            for k in range(CONV_SIZE):
                o = off + k
                a0 = (o // SUBLANES) * SUBLANES + r0
                acc = acc + (us_ref[o % SUBLANES, a0:a0 + CONV_ROWS, pl.ds(l0, LANES)]
                             * wc_ref[k:k + 1, pl.ds(l0, LANES)])
            y_ref[r0:r0 + CONV_ROWS, pl.ds(l0, LANES)] = acc
        return carry

    lax.fori_loop(0, cw // LANES, conv_lanes, 0)

    y = y_ref[...] + cb_ref[...]
    mu = jnp.mean(y, axis=-1, keepdims=True)
    yc = y - mu
    var = jnp.mean(yc * yc, axis=-1, keepdims=True)
    yn = yc * lax.rsqrt(var + LN_EPS) * lg_ref[...] + lb_ref[...]
    conv = (yn * jax.nn.sigmoid(yn)).astype(BF16)

    aw = attn_ref.shape[1]
    mixed = (jnp.dot(attn_ref[...], wo_ref[0:aw, :], preferred_element_type=F32)
             + jnp.dot(conv, wo_ref[aw:, :], preferred_element_type=F32) + bo_ref[...])
    o_ref[...] = x_ref[...] + gate_ref[0] * mixed


def _mix(ab, attn, x2d, conv_w, conv_b, ln_g, ln_b, w_out, b_out, gate1, nb, seq, tt=512):
    t, d = x2d.shape
    cw = conv_w.shape[1]
    aw = attn.shape[1]
    ab3 = ab.reshape(nb, seq, 2 * cw)
    attn3 = attn.reshape(nb, seq, aw)
    x3 = x2d.reshape(nb, seq, d)
    hsub = tt // CONV_HALO
    out = pl.pallas_call(
        _mix_kernel,
        out_shape=jax.ShapeDtypeStruct((nb, seq, d), F32),
        grid=(nb, seq // tt),
        in_specs=[
            pl.BlockSpec((None, tt, 2 * cw), lambda b, i: (b, i, 0)),
            pl.BlockSpec((None, CONV_HALO, 2 * cw), lambda b, i: (b, jnp.maximum(i * hsub - 1, 0), 0)),
            pl.BlockSpec((None, tt, aw), lambda b, i: (b, i, 0)),
            pl.BlockSpec((None, tt, d), lambda b, i: (b, i, 0)),
            pl.BlockSpec((CONV_SIZE, cw), lambda b, i: (0, 0)),
            pl.BlockSpec((1, cw), lambda b, i: (0, 0)),
            pl.BlockSpec((1, cw), lambda b, i: (0, 0)),
            pl.BlockSpec((1, cw), lambda b, i: (0, 0)),
            pl.BlockSpec((d, d), lambda b, i: (0, 0), pipeline_mode=pl.Buffered(1)),
            pl.BlockSpec((1, d), lambda b, i: (0, 0)),
            pl.BlockSpec((1, 1, d), lambda b, i: (b, 0, 0)),
        ],
        out_specs=pl.BlockSpec((None, tt, d), lambda b, i: (b, i, 0)),
        scratch_shapes=[pltpu.VMEM((SUBLANES, CONV_HALO + tt, cw), F32), pltpu.VMEM((tt, cw), F32)],
        compiler_params=_params(("arbitrary", "arbitrary")),
        name="mix",
    )(ab3, ab3, attn3, x3, conv_w, conv_b.reshape(1, cw), ln_g.reshape(1, cw), ln_b.reshape(1, cw),
      w_out.astype(BF16), b_out.reshape(1, d), gate1.reshape(nb, 1, d))
    return out.reshape(t, d)


PACK_SUB = SUBLANES
HI_MASK = 0xFFFF0000


def _pack_rows(v):
    return _pack_rounded(v.astype(BF16).astype(F32))


def _pack_rounded(v):
    half = v.shape[1] // 2
    assert half == PACK_SUB * LANES
    bits = lax.bitcast_convert_type(v, jnp.uint32)
    word = (bits[:, :half] >> 16) | (bits[:, half:] & jnp.uint32(HI_MASK))
    planes = jnp.stack([word[:, s * LANES:(s + 1) * LANES] for s in range(PACK_SUB)], axis=0)
    return pltpu.einshape("smr->msr", planes)


def _unpack_rows(p):
    planes = pltpu.einshape("msr->smr", p)
    lo = [lax.bitcast_convert_type(planes[s] << 16, F32) for s in range(PACK_SUB)]
    hi = [lax.bitcast_convert_type(planes[s] & jnp.uint32(HI_MASK), F32) for s in range(PACK_SUB)]
    return jnp.concatenate(lo + hi, axis=1)


def _router_kernel(x_ref, g_ref, sh_ref, sc_ref, whi_ref, wlo_ref, b_ref, h_ref, r_ref):
    h = _rmsnorm_mod(x_ref[...], g_ref[...], sh_ref[0], sc_ref[0])
    hi = h.astype(BF16)
    hi_f32 = hi.astype(F32)
    h_ref[...] = _pack_rounded(hi_f32)
    lo = (h - hi_f32).astype(BF16)
    logits = (jnp.dot(hi, whi_ref[...], preferred_element_type=F32)
              + jnp.dot(hi, wlo_ref[...], preferred_element_type=F32)
              + jnp.dot(lo, whi_ref[...], preferred_element_type=F32)) + b_ref[...]
    tm = logits.shape[0]
    lane = lax.broadcasted_iota(jnp.int32, (tm, LANES), 1)
    big = jnp.int32(LANES)

    def first_argmax(v):
        m = jnp.max(v, axis=-1, keepdims=True)
        idx = jnp.min(jnp.where(v == m, lane, big), axis=-1, keepdims=True)
        return m, idx

    is_group = lane < N_GROUPS
    gl = jnp.where(is_group, logits, NEG_INF)
    gmax, gsel = first_argmax(gl)
    p_g = 1.0 / jnp.sum(jnp.where(is_group, jnp.exp(gl - gmax), 0.0), axis=-1, keepdims=True)
    e_lo = N_GROUPS + gsel * EXPERTS_PER_GROUP
    in_group = (lane >= e_lo) & (lane < e_lo + EXPERTS_PER_GROUP)
    el = jnp.where(in_group, logits, NEG_INF)
    t1, i1 = first_argmax(el)
    el2 = jnp.where(lane == i1, NEG_INF, el)
    t2, i2 = first_argmax(el2)
    e2 = jnp.exp(t2 - t1)
    w1 = p_g / (1.0 + e2)
    w2 = p_g * e2 / (1.0 + e2)
    id1 = (i1 - N_GROUPS).astype(F32)
    id2 = (i2 - N_GROUPS).astype(F32)
    r_ref[...] = jnp.where(lane == 0, id1, jnp.where(lane == 1, id2,
                           jnp.where(lane == 2, w1, jnp.where(lane == 3, w2, 0.0))))


def _router(x2, g_ffn, shift2, scale2, w_gr, b_gr, w_er, b_er, seq, tm=1024):
    t, d = x2.shape
    nb = shift2.shape[0]
    pad = LANES - N_GROUPS - N_EXPERTS
    w = jnp.concatenate([w_gr, w_er, jnp.zeros((d, pad), F32)], axis=1)
    bias = jnp.concatenate([b_gr, b_er, jnp.zeros((pad,), F32)]).reshape(1, LANES)
    whi = w.astype(BF16)
    wlo = (w - whi.astype(F32)).astype(BF16)
    steps_per_seq = seq // tm
    bvec = lambda i: (i // steps_per_seq, 0, 0)
    return pl.pallas_call(
        _router_kernel,
        out_shape=(jax.ShapeDtypeStruct((t, PACK_SUB, LANES), jnp.uint32),
                   jax.ShapeDtypeStruct((t, LANES), F32)),
        grid=(t // tm,),
        in_specs=[
            pl.BlockSpec((tm, d), lambda i: (i, 0)),
            pl.BlockSpec((1, d), lambda i: (0, 0)),
            pl.BlockSpec((1, 1, d), bvec),
            pl.BlockSpec((1, 1, d), bvec),
            pl.BlockSpec((d, LANES), lambda i: (0, 0)),
            pl.BlockSpec((d, LANES), lambda i: (0, 0)),
            pl.BlockSpec((1, LANES), lambda i: (0, 0)),
        ],
        out_specs=(pl.BlockSpec((tm, PACK_SUB, LANES), lambda i: (i, 0, 0)),
                   pl.BlockSpec((tm, LANES), lambda i: (i, 0))),
        compiler_params=_params(("arbitrary",)),
        name="router",
    )(x2, g_ffn.reshape(1, d), shift2.reshape(nb, 1, d), scale2.reshape(nb, 1, d), whi, wlo, bias)


def _final_kernel(x_ref, y0_ref, y1_ref, r_ref, gate_ref, g_ref, o_ref):
    w0 = r_ref[:, 2:3]
    w1 = r_ref[:, 3:4]
    ffn = w0 * _unpack_rows(y0_ref[...]) + w1 * _unpack_rows(y1_ref[...])
    x = x_ref[...] + gate_ref[0] * ffn
    ms = jnp.mean(x * x, axis=-1, keepdims=True)
    o_ref[...] = (x * lax.rsqrt(ms + RMS_EPS)) * g_ref[...]


def _final(x2, ys, route, gate2, g_final, seq, tm=1024):
    t, d = x2.shape
    nb = gate2.shape[0]
    steps_per_seq = seq // tm
    return pl.pallas_call(
        _final_kernel,
        out_shape=jax.ShapeDtypeStruct((t, d), F32),
        grid=(t // tm,),
        in_specs=[
            pl.BlockSpec((tm, d), lambda i: (i, 0)),
            pl.BlockSpec((tm, PACK_SUB, LANES), lambda i: (i, 0, 0)),
            pl.BlockSpec((tm, PACK_SUB, LANES), lambda i: (i + t // tm, 0, 0)),
            pl.BlockSpec((tm, LANES), lambda i: (i, 0)),
            pl.BlockSpec((1, 1, d), lambda i: (i // steps_per_seq, 0, 0)),
            pl.BlockSpec((1, d), lambda i: (0, 0)),
        ],
        out_specs=pl.BlockSpec((tm, d), lambda i: (i, 0)),
        compiler_params=_params(("arbitrary",)),
        name="final",
    )(x2, ys, ys, route, gate2.reshape(nb, 1, d), g_final.reshape(1, d))


MOE_ROWS = 256
ROW_BUFS = 3
BLK_LEAD = 1
BLK_TRAIL = 2
WEIGHT_DMA_PRIORITY = 1


def _moe_kernel(blk_e, blk_ord, exp_list, n_active, blk_n, blk_i0, src_tok, dst_row,
                h_hbm, wgu_hbm, wd_hbm, ys_hbm,
                h0, h1, h2, o0, o1, o2, wgu_st, wd_st, wgu_bf, wd_bf, gsem, ssem, wsem):
    b = pl.program_id(0)
    tm = h0.shape[0]
    dump0 = ys_hbm.shape[0] - ROW_BUFS * tm
    hbufs, obufs = (h0, h1, h2), (o0, o1, o2)

    def count(blk):
        return blk_n[blk + BLK_LEAD]

    def first(blk):
        return blk_i0[blk + BLK_LEAD]

    def gather_row(i0, j, s):
        return pltpu.make_async_copy(h_hbm.at[src_tok[i0 + j]], hbufs[s].at[j], gsem.at[s])

    def scatter_row(i0, n_rows, j, s):
        row = jnp.where(j < n_rows, dst_row[i0 + j], dump0 + s * tm + j)
        return pltpu.make_async_copy(obufs[s].at[j], ys_hbm.at[row], ssem.at[s])

    def wait_gather(s):
        pltpu.make_async_copy(h_hbm.at[pl.ds(0, tm)], hbufs[s], gsem.at[s]).wait()

    def wait_scatter(s):
        pltpu.make_async_copy(obufs[s], ys_hbm.at[pl.ds(0, tm)], ssem.at[s]).wait()

    def loop_rows(start_row):
        def body(j, carry):
            start_row(j)
            return carry
        lax.fori_loop(0, tm, body, 0, unroll=8)

    @pl.when(b == 0)
    def _():
        for s in range(ROW_BUFS):
            obufs[s][...] = jnp.zeros(obufs[s].shape, obufs[s].dtype)
            dump = pltpu.make_async_copy(obufs[s], ys_hbm.at[pl.ds(dump0 + s * tm, tm)], ssem.at[s])
            dump.start()
            dump.wait()
        for blk in range(ROW_BUFS - 1):
            i0 = first(blk)
            loop_rows(lambda j, i0=i0, blk=blk: gather_row(i0, j, blk).start())

    def weight_copies(ordinal, slot):
        e = exp_list[ordinal]
        return (pltpu.make_async_copy(wgu_hbm.at[e], wgu_st.at[slot], wsem.at[slot, 0]),
                pltpu.make_async_copy(wd_hbm.at[e], wd_st.at[slot], wsem.at[slot, 1]))

    @pl.when(b == 0)
    def _():
        for ordinal in range(2):
            @pl.when(ordinal < n_active[0])
            def _():
                for cp in weight_copies(ordinal, ordinal):
                    cp.start(priority=WEIGHT_DMA_PRIORITY)

    @pl.when(count(b) > 0)
    def _():
        prev_e = blk_e[jnp.maximum(b - 1, 0)]

        @pl.when(jnp.logical_or(b == 0, blk_e[b] != prev_e))
        def _():
            ordinal = blk_ord[b]
            slot = ordinal % 2
            for cp in weight_copies(ordinal, slot):
                cp.wait()
            wgu_bf[...] = wgu_st[slot].astype(BF16)
            wd_bf[...] = wd_st[slot].astype(BF16)

            @pl.when(ordinal + 2 < n_active[0])
            def _():
                for cp in weight_copies(ordinal + 2, slot):
                    cp.start(priority=WEIGHT_DMA_PRIORITY)

        def block_step(s):
            t = (s + ROW_BUFS - 1) % ROW_BUFS
            u = (s + 1) % ROW_BUFS
            wait_gather(s)

            @pl.when(b >= ROW_BUFS - 1)
            def _():
                wait_scatter(s)

            i0_next, i0_prev, n_prev = first(b + 2), first(b - 1), count(b - 1)
            for j in range(tm):
                gather_row(i0_next, j, t).start()
                scatter_row(i0_prev, n_prev, j, t).start()
            h = _unpack_rows(hbufs[s][...]).astype(BF16)
            gu = jnp.dot(h, wgu_bf[...], preferred_element_type=F32)
            ff = gu.shape[1] // 2
            gate, up = gu[:, :ff], gu[:, ff:]
            act = (gate * jax.nn.sigmoid(gate) * up).astype(BF16)
            obufs[s][...] = _pack_rows(jnp.dot(act, wd_bf[...], preferred_element_type=F32))

            @pl.when(count(b + 1) == 0)
            def _():
                i0, n_rows = first(b), count(b)
                loop_rows(lambda j: scatter_row(i0, n_rows, j, s).start())

                @pl.when(b >= 1)
                def _():
                    wait_scatter(u)
                wait_scatter(t)
                wait_scatter(s)
                wait_gather(u)
                wait_gather(t)

        for s in range(ROW_BUFS):
            pl.when(b % ROW_BUFS == s)(functools.partial(block_step, s))


def _moe_plan(route, tm, t):
    ids = route[:, 0:TOP_K].astype(jnp.int32)
    flat_e = ids.reshape(-1)
    n_assign = flat_e.shape[0]
    order = jnp.argsort(flat_e).astype(jnp.int32)
    counts = jnp.sum(flat_e[:, None] == jnp.arange(N_EXPERTS, dtype=jnp.int32)[None, :], axis=0,
                     dtype=jnp.int32)
    raw_start = jnp.cumsum(counts) - counts
    nblk = (counts + tm - 1) // tm
    blk_end = jnp.cumsum(nblk)
    n_blocks = -(-(n_assign + N_EXPERTS * (tm - 1)) // tm)
    bidx = jnp.arange(n_blocks, dtype=jnp.int32)
    total = blk_end[-1]
    e_of = jnp.sum(jnp.minimum(bidx, total - 1)[:, None] >= blk_end[None, :], axis=1, dtype=jnp.int32)
    e_of = jnp.minimum(e_of, N_EXPERTS - 1)
    j_in = bidx - (blk_end[e_of] - nblk[e_of])
    blk_i0 = raw_start[e_of] + j_in * tm
    blk_n = jnp.where(bidx < total, jnp.clip(counts[e_of] - j_in * tm, 0, tm), 0).astype(jnp.int32)
    blk_i0 = jnp.where(bidx < total, blk_i0, 0).astype(jnp.int32)
    lead, trail = jnp.zeros((BLK_LEAD,), jnp.int32), jnp.zeros((BLK_TRAIL,), jnp.int32)
    blk_n = jnp.concatenate([lead, blk_n, trail])
    blk_i0 = jnp.concatenate([lead, blk_i0, trail])
    tok, k = order // TOP_K, order % TOP_K
    spare = jnp.arange(tm, dtype=jnp.int32) % t
    src_tok = jnp.concatenate([tok, spare])
    dst_row = jnp.concatenate([k * t + tok, spare])
    active = counts > 0
    ordinal_of = (jnp.cumsum(active) - 1).astype(jnp.int32)
    blk_ord = ordinal_of[e_of]
    exp_list = jnp.argsort(jnp.logical_not(active), stable=True).astype(jnp.int32)
    n_active = jnp.sum(active, dtype=jnp.int32).reshape(1)
    return e_of, blk_ord, exp_list, n_active, blk_n, blk_i0, src_tok, dst_row, n_blocks


def _moe(h2p, route, w_gate_up, w_down, tm=MOE_ROWS):
    t = h2p.shape[0]
    d, ff2 = w_gate_up.shape[1], w_gate_up.shape[2]
    n_assign = t * TOP_K
    *tables, n_blocks = _moe_plan(route, tm, t)
    row_buf = pltpu.VMEM((tm, PACK_SUB, LANES), jnp.uint32)
    return pl.pallas_call(
        _moe_kernel,
        out_shape=jax.ShapeDtypeStruct((n_assign + ROW_BUFS * tm, PACK_SUB, LANES), jnp.uint32),
        grid_spec=pltpu.PrefetchScalarGridSpec(
            num_scalar_prefetch=len(tables),
            grid=(n_blocks,),
            in_specs=[pl.BlockSpec(memory_space=pl.ANY)] * 3,
            out_specs=pl.BlockSpec(memory_space=pl.ANY),
            scratch_shapes=[row_buf] * (2 * ROW_BUFS) + [
                            pltpu.VMEM((2, d, ff2), F32), pltpu.VMEM((2, ff2 // 2, d), F32),
                            pltpu.VMEM((d, ff2), BF16), pltpu.VMEM((ff2 // 2, d), BF16),
                            pltpu.SemaphoreType.DMA((ROW_BUFS,)), pltpu.SemaphoreType.DMA((ROW_BUFS,)),
                            pltpu.SemaphoreType.DMA((2, 2))],
        ),
        compiler_params=_params(("arbitrary",)),
        name="moe",
    )(*tables, h2p, w_gate_up, w_down)


def kernel(x, c, positions, w_ada, b_ada, g_mix, w_in, b_in, attn_sinks, conv_w, conv_b, conv_ln_g,
           conv_ln_b, w_out, b_out, g_ffn, w_group_router, b_group_router, w_expert_router,
           b_expert_router, w_gate_up, w_down, g_final):
    nb, seq, d = x.shape
    t = nb * seq
    mod = _ada(c, w_ada, b_ada)
    shift1, scale1, gate1, shift2, scale2, gate2 = [mod[:, i * d:(i + 1) * d] for i in range(6)]
    x2d = x.reshape(t, d)
    q, kv, ab = _inproj(x2d, positions.reshape(t, 1), g_mix, shift1, scale1, w_in, b_in, seq)
    attn = _attn(q, kv, attn_sinks, nb, seq)
    x2 = _mix(ab, attn, x2d, conv_w, conv_b, conv_ln_g, conv_ln_b, w_out, b_out, gate1, nb, seq)
    h2, route = _router(x2, g_ffn, shift2, scale2, w_group_router, b_group_router,
                        w_expert_router, b_expert_router, seq)
    ys = _moe(h2, route, w_gate_up, w_down)
    out = _final(x2, ys, route, gate2, g_final, seq)
    return out.reshape(nb, seq, d)
```

```python
import functools
import math

import jax
import jax.numpy as jnp
from jax import lax
from jax.experimental import pallas as pl
from jax.experimental.pallas import tpu as pltpu

F32 = jnp.float32
BF16 = jnp.bfloat16

HEAD_DIM = 64
N_Q_HEADS = 16
N_KV_HEADS = 2
ATTN_WIDTH = N_Q_HEADS * HEAD_DIM
KV_WIDTH = N_KV_HEADS * HEAD_DIM
CONV_SIZE = 31
WINDOW = 128
ROPE_THETA = 10000.0
N_GROUPS = 8
EXPERTS_PER_GROUP = 8
N_EXPERTS = N_GROUPS * EXPERTS_PER_GROUP
TOP_K = 2
EXPERT_FF = 512
RMS_EPS = 1e-6
LN_EPS = 1e-5
NEG_INF = -1e30

LANES = 128
SUBLANES = 8
VMEM_LIMIT = 56 * 1024 * 1024


def _params(semantics, vmem=VMEM_LIMIT):
    return pltpu.CompilerParams(dimension_semantics=semantics, vmem_limit_bytes=vmem)


def _ada_kernel(cb_ref, w_ref, b_ref, o_ref, s_ref):
    nb, d = cb_ref.shape[0], w_ref.shape[0]
    nl = w_ref.shape[1] // LANES
    cv = cb_ref[...]
    s_ref[...] = cv * jax.nn.sigmoid(cv)

    def body(k, acc):
        r = pl.multiple_of(k * SUBLANES, SUBLANES)
        w8 = w_ref[pl.ds(r, SUBLANES), :]
        new = []
        for b in range(nb):
            sv = s_ref[b, pl.ds(r, SUBLANES), :]
            for j in range(nl):
                new.append(acc[b * nl + j] + w8[:, j * LANES:(j + 1) * LANES] * sv)
        return tuple(new)

    init = tuple(jnp.zeros((SUBLANES, LANES), F32) for _ in range(nb * nl))
    acc = lax.fori_loop(0, d // SUBLANES, body, init, unroll=4)
    for b in range(nb):
        row = jnp.concatenate(
            [jnp.sum(acc[b * nl + j], axis=0, keepdims=True) for j in range(nl)], axis=1)
        o_ref[b:b + 1, :] = row + b_ref[...]


def _ada(c, w_ada, b_ada, tn=1024):
    nb, d = c.shape
    n = w_ada.shape[1]
    cb = jnp.broadcast_to(c[:, :, None], (nb, d, LANES))
    return pl.pallas_call(
        _ada_kernel,
        out_shape=jax.ShapeDtypeStruct((nb, n), F32),
        grid=(n // tn,),
        in_specs=[
            pl.BlockSpec((nb, d, LANES), lambda j: (0, 0, 0)),
            pl.BlockSpec((d, tn), lambda j: (0, j)),
            pl.BlockSpec((1, tn), lambda j: (0, j)),
        ],
        out_specs=pl.BlockSpec((nb, tn), lambda j: (0, j)),
        scratch_shapes=[pltpu.VMEM((nb, d, LANES), F32)],
        compiler_params=_params(("arbitrary",)),
        name="ada",
    )(cb, w_ada, b_ada.reshape(1, n))


def _rmsnorm_mod(x, g, shift, scale):
    ms = jnp.mean(x * x, axis=-1, keepdims=True)
    h = (x * lax.rsqrt(ms + RMS_EPS)) * g
    return h * (1.0 + scale) + shift


def _inproj_kernel(x_ref, pos_ref, g_ref, sh_ref, sc_ref, w_ref, b_ref, invf_ref, sgn_ref, cj_ref, sj_ref,
                   q_ref, kv_ref, ab_ref):
    h = _rmsnorm_mod(x_ref[...], g_ref[...], sh_ref[0], sc_ref[0]).astype(BF16)

    a0 = pos_ref[0:1, :].astype(F32) * invf_ref[...]
    c0, s0 = jnp.cos(a0), jnp.sin(a0)
    cj, sj = cj_ref[...], sj_ref[...]
    cosv = c0 * cj - s0 * sj
    sinv = (s0 * cj + c0 * sj) * sgn_ref[...]
    lane = lax.broadcasted_iota(jnp.int32, (1, LANES), 1)
    first_half = (lane % HEAD_DIM) < (HEAD_DIM // 2)

    def rope(t):
        rot = jnp.where(first_half, pltpu.roll(t, LANES - HEAD_DIM // 2, 1),
                        pltpu.roll(t, HEAD_DIM // 2, 1))
        return t * cosv + rot * sinv

    def proj(c0, width):
        return (jnp.dot(h, w_ref[:, c0:c0 + width], preferred_element_type=F32)
                + b_ref[:, c0:c0 + width])

    qscale = 1.0 / math.sqrt(HEAD_DIM)
    step = 512
    for c0 in range(0, ATTN_WIDTH, step):
        z = proj(c0, step)
        for j in range(step // LANES):
            q_ref[:, c0 + j * LANES:c0 + (j + 1) * LANES] = (
                rope(z[:, j * LANES:(j + 1) * LANES]) * qscale).astype(BF16)
    z = proj(ATTN_WIDTH, 2 * KV_WIDTH)
    k = rope(z[:, :KV_WIDTH])
    v = z[:, KV_WIDTH:]
    kv_ref[:, 0 * LANES:1 * LANES] = k.astype(BF16)
    kv_ref[:, 1 * LANES:2 * LANES] = pltpu.roll(k, HEAD_DIM, 1).astype(BF16)
    kv_ref[:, 2 * LANES:3 * LANES] = v.astype(BF16)
    kv_ref[:, 3 * LANES:4 * LANES] = pltpu.roll(v, HEAD_DIM, 1).astype(BF16)
    base = ATTN_WIDTH + 2 * KV_WIDTH
    for c0 in range(0, ab_ref.shape[1], step):
        ab_ref[:, c0:c0 + step] = proj(base + c0, step).astype(BF16)


def _inproj(x2d, pos2d, g_mix, shift1, scale1, w_in, b_in, seq, tm=512):
    t, d = x2d.shape
    n = w_in.shape[1]
    nb = shift1.shape[0]
    half = HEAD_DIM // 2
    inv_freq = ROPE_THETA ** (-jnp.arange(half, dtype=F32) * 2.0 / HEAD_DIM)
    invf = jnp.tile(inv_freq, LANES // half).reshape(1, LANES)
    sgn = jnp.tile(jnp.concatenate([-jnp.ones((half,), F32), jnp.ones((half,), F32)]),
                   LANES // HEAD_DIM).reshape(1, LANES)
    steps_per_seq = seq // tm
    conv_w2 = n - ATTN_WIDTH - 2 * KV_WIDTH
    bvec = lambda i: (i // steps_per_seq, 0, 0)
    ang_j = jnp.arange(tm, dtype=F32)[:, None] * invf
    cj, sj = jnp.cos(ang_j), jnp.sin(ang_j)
    return pl.pallas_call(
        _inproj_kernel,
        out_shape=(jax.ShapeDtypeStruct((t, ATTN_WIDTH), BF16),
                   jax.ShapeDtypeStruct((t, 4 * KV_WIDTH), BF16),
                   jax.ShapeDtypeStruct((t, conv_w2), BF16)),
        grid=(t // tm,),
        in_specs=[
            pl.BlockSpec((tm, d), lambda i: (i, 0)),
            pl.BlockSpec((tm, 1), lambda i: (i, 0)),
            pl.BlockSpec((1, d), lambda i: (0, 0)),
            pl.BlockSpec((1, 1, d), bvec),
            pl.BlockSpec((1, 1, d), bvec),
            pl.BlockSpec((d, n), lambda i: (0, 0)),
            pl.BlockSpec((1, n), lambda i: (0, 0)),
            pl.BlockSpec((1, LANES), lambda i: (0, 0)),
            pl.BlockSpec((1, LANES), lambda i: (0, 0)),
            pl.BlockSpec((tm, LANES), lambda i: (0, 0)),
            pl.BlockSpec((tm, LANES), lambda i: (0, 0)),
        ],
        out_specs=(pl.BlockSpec((tm, ATTN_WIDTH), lambda i: (i, 0)),
                   pl.BlockSpec((tm, 4 * KV_WIDTH), lambda i: (i, 0)),
                   pl.BlockSpec((tm, conv_w2), lambda i: (i, 0))),
        compiler_params=_params(("arbitrary",)),
        name="inproj",
    )(x2d, pos2d, g_mix.reshape(1, d), shift1.reshape(nb, 1, d), scale1.reshape(nb, 1, d),
      w_in.astype(BF16), b_in.reshape(1, n), invf, sgn, cj, sj)


def _attn_kernel(sink_ref, q_ref, kvm_ref, kvh_ref, o_ref):
    i = pl.program_id(1)
    tq = q_ref.shape[0]
    blk = WINDOW
    heads_per_group = N_Q_HEADS // N_KV_HEADS
    pairs = heads_per_group // 2
    row = lax.broadcasted_iota(jnp.int32, (blk, 2 * blk), 0)
    col = lax.broadcasted_iota(jnp.int32, (blk, 2 * blk), 1)
    rel = row + blk - col
    band = (rel >= 0) & (rel < WINDOW)
    lane = lax.broadcasted_iota(jnp.int32, (1, LANES), 1)
    lo = lane < HEAD_DIM
    zero = jnp.zeros((), BF16)

    for j in range(tq // blk):
        if j == 0:
            kv = jnp.concatenate([kvh_ref[...], kvm_ref[0:blk, :]], axis=0)
            valid = band & ((col >= blk) | (i > 0))
        else:
            kv = kvm_ref[(j - 1) * blk:(j + 1) * blk, :]
            valid = band
        k_nat, k_swp = kv[:, 0:LANES], kv[:, LANES:2 * LANES]
        v_nat, v_swp = kv[:, 2 * LANES:3 * LANES], kv[:, 3 * LANES:4 * LANES]
        for g in range(N_KV_HEADS):
            src_lo_k, src_hi_k = (k_nat, k_swp) if g == 0 else (k_swp, k_nat)
            src_lo_v, src_hi_v = (v_nat, v_swp) if g == 0 else (v_swp, v_nat)
            k_even = jnp.where(lo, src_lo_k, zero)
            k_odd = jnp.where(lo, zero, src_hi_k)
            v_even = jnp.where(lo, src_lo_v, zero)
            v_odd = jnp.where(lo, zero, src_hi_v)
            qs = jnp.concatenate(
                [q_ref[j * blk:(j + 1) * blk, (g * pairs + p) * LANES:(g * pairs + p + 1) * LANES]
                 for p in range(pairs)], axis=0)
            nt = (((1,), (1,)), ((), ()))
            s_even = lax.dot_general(qs, k_even, nt, preferred_element_type=F32)
            s_odd = lax.dot_general(qs, k_odd, nt, preferred_element_type=F32)
            probs, rinv = [], []
            for s_all, parity in ((s_even, 0), (s_odd, 1)):
                ps, rs = [], []
                for p in range(pairs):
                    sink = sink_ref[g * heads_per_group + 2 * p + parity]
                    s = jnp.where(valid, s_all[p * blk:(p + 1) * blk, :], NEG_INF)
                    m = jnp.maximum(jnp.max(s, axis=-1, keepdims=True), sink)
                    e = jnp.exp(s - m)
                    den = jnp.sum(e, axis=-1, keepdims=True) + jnp.exp(sink - m)
                    ps.append(e.astype(BF16))
                    rs.append(1.0 / den)
                probs.append(jnp.concatenate(ps, axis=0))
                rinv.append(rs)
            o = (jnp.dot(probs[0], v_even, preferred_element_type=F32)
                 + jnp.dot(probs[1], v_odd, preferred_element_type=F32))
            for p in range(pairs):
                scale = jnp.where(lo, rinv[0][p], rinv[1][p])
                c = (g * pairs + p) * LANES
                o_ref[j * blk:(j + 1) * blk, c:c + LANES] = (
                    o[p * blk:(p + 1) * blk, :] * scale).astype(BF16)


def _attn(q, kv, sinks, nb, seq, tq=512):
    blk = WINDOW
    q3 = q.reshape(nb, seq, ATTN_WIDTH)
    kv3 = kv.reshape(nb, seq, 4 * KV_WIDTH)
    sub = tq // blk
    out = pl.pallas_call(
        _attn_kernel,
        out_shape=jax.ShapeDtypeStruct((nb, seq, ATTN_WIDTH), BF16),
        grid=(nb, seq // tq),
        in_specs=[
            pl.BlockSpec(memory_space=pltpu.SMEM),
            pl.BlockSpec((None, tq, ATTN_WIDTH), lambda b, i: (b, i, 0)),
            pl.BlockSpec((None, tq, 4 * KV_WIDTH), lambda b, i: (b, i, 0)),
            pl.BlockSpec((None, blk, 4 * KV_WIDTH), lambda b, i: (b, jnp.maximum(i * sub - 1, 0), 0)),
        ],
        out_specs=pl.BlockSpec((None, tq, ATTN_WIDTH), lambda b, i: (b, i, 0)),
        compiler_params=_params(("arbitrary", "arbitrary")),
        name="attn",
    )(sinks.astype(F32), q3, kv3, kv3)
    return out.reshape(nb * seq, ATTN_WIDTH)


CONV_HALO = 32
CONV_ROWS = 256


def _mix_kernel(ab_ref, abh_ref, attn_ref, x_ref, wc_ref, cb_ref, lg_ref, lb_ref, wo_ref, bo_ref,
                gate_ref, o_ref, us_ref, y_ref):
    i = pl.program_id(1)
    tt = x_ref.shape[0]
    cw = y_ref.shape[1]

    def glu(ab):
        a = ab[:, :cw].astype(F32)
        b = ab[:, cw:].astype(F32)
        return a * jax.nn.sigmoid(b)

    halo = glu(abh_ref[...])
    us_ref[0, 0:CONV_HALO, :] = jnp.where(i > 0, halo, 0.0)
    us_ref[0, CONV_HALO:, :] = glu(ab_ref[...])
    n = tt + CONV_HALO - SUBLANES
    for s in range(1, SUBLANES):
        us_ref[s, 0:n, :] = us_ref[0, s:s + n, :]

    off = CONV_HALO - (CONV_SIZE - 1)

    def conv_lanes(c, carry):
        l0 = pl.multiple_of(c * LANES, LANES)
        for r0 in range(0, tt, CONV_ROWS):
            acc = jnp.zeros((CONV_ROWS, LANES), F32)
            for k in range(CONV_SIZE):
                o = off + k
                a0 = (o // SUBLANES) * SUBLANES + r0
                acc = acc + (us_ref[o % SUBLANES, a0:a0 + CONV_ROWS, pl.ds(l0, LANES)]
                             * wc_ref[k:k + 1, pl.ds(l0, LANES)])
            y_ref[r0:r0 + CONV_ROWS, pl.ds(l0, LANES)] = acc
        return carry

    lax.fori_loop(0, cw // LANES, conv_lanes, 0)

    y = y_ref[...] + cb_ref[...]
    mu = jnp.mean(y, axis=-1, keepdims=True)
    yc = y - mu
    var = jnp.mean(yc * yc, axis=-1, keepdims=True)
    yn = yc * lax.rsqrt(var + LN_EPS) * lg_ref[...] + lb_ref[...]
    conv = (yn * jax.nn.sigmoid(yn)).astype(BF16)

    aw = attn_ref.shape[1]
    mixed = (jnp.dot(attn_ref[...], wo_ref[0:aw, :], preferred_element_type=F32)
             + jnp.dot(conv, wo_ref[aw:, :], preferred_element_type=F32) + bo_ref[...])
    o_ref[...] = x_ref[...] + gate_ref[0] * mixed


def _mix(ab, attn, x2d, conv_w, conv_b, ln_g, ln_b, w_out, b_out, gate1, nb, seq, tt=512):
    t, d = x2d.shape
    cw = conv_w.shape[1]
    aw = attn.shape[1]
    ab3 = ab.reshape(nb, seq, 2 * cw)
    attn3 = attn.reshape(nb, seq, aw)
    x3 = x2d.reshape(nb, seq, d)
    hsub = tt // CONV_HALO
    out = pl.pallas_call(
        _mix_kernel,
        out_shape=jax.ShapeDtypeStruct((nb, seq, d), F32),
        grid=(nb, seq // tt),
        in_specs=[
            pl.BlockSpec((None, tt, 2 * cw), lambda b, i: (b, i, 0)),
            pl.BlockSpec((None, CONV_HALO, 2 * cw), lambda b, i: (b, jnp.maximum(i * hsub - 1, 0), 0)),
            pl.BlockSpec((None, tt, aw), lambda b, i: (b, i, 0)),
            pl.BlockSpec((None, tt, d), lambda b, i: (b, i, 0)),
            pl.BlockSpec((CONV_SIZE, cw), lambda b, i: (0, 0)),
            pl.BlockSpec((1, cw), lambda b, i: (0, 0)),
            pl.BlockSpec((1, cw), lambda b, i: (0, 0)),
            pl.BlockSpec((1, cw), lambda b, i: (0, 0)),
            pl.BlockSpec((d, d), lambda b, i: (0, 0), pipeline_mode=pl.Buffered(1)),
            pl.BlockSpec((1, d), lambda b, i: (0, 0)),
            pl.BlockSpec((1, 1, d), lambda b, i: (b, 0, 0)),
        ],
        out_specs=pl.BlockSpec((None, tt, d), lambda b, i: (b, i, 0)),
        scratch_shapes=[pltpu.VMEM((SUBLANES, CONV_HALO + tt, cw), F32), pltpu.VMEM((tt, cw), F32)],
        compiler_params=_params(("arbitrary", "arbitrary")),
        name="mix",
    )(ab3, ab3, attn3, x3, conv_w, conv_b.reshape(1, cw), ln_g.reshape(1, cw), ln_b.reshape(1, cw),
      w_out.astype(BF16), b_out.reshape(1, d), gate1.reshape(nb, 1, d))
    return out.reshape(t, d)


PACK_SUB = SUBLANES
HI_MASK = 0xFFFF0000


def _pack_rows(v):
    return _pack_rounded(v.astype(BF16).astype(F32))


def _pack_rounded(v):
    half = v.shape[1] // 2
    assert half == PACK_SUB * LANES
    bits = lax.bitcast_convert_type(v, jnp.uint32)
    word = (bits[:, :half] >> 16) | (bits[:, half:] & jnp.uint32(HI_MASK))
    planes = jnp.stack([word[:, s * LANES:(s + 1) * LANES] for s in range(PACK_SUB)], axis=0)
    return pltpu.einshape("smr->msr", planes)


def _unpack_rows(p):
    planes = pltpu.einshape("msr->smr", p)
    lo = [lax.bitcast_convert_type(planes[s] << 16, F32) for s in range(PACK_SUB)]
    hi = [lax.bitcast_convert_type(planes[s] & jnp.uint32(HI_MASK), F32) for s in range(PACK_SUB)]
    return jnp.concatenate(lo + hi, axis=1)


def _router_kernel(x_ref, g_ref, sh_ref, sc_ref, whi_ref, wlo_ref, b_ref, h_ref, r_ref):
    h = _rmsnorm_mod(x_ref[...], g_ref[...], sh_ref[0], sc_ref[0])
    hi = h.astype(BF16)
    hi_f32 = hi.astype(F32)
    h_ref[...] = _pack_rounded(hi_f32)
    lo = (h - hi_f32).astype(BF16)
    logits = (jnp.dot(hi, whi_ref[...], preferred_element_type=F32)
              + jnp.dot(hi, wlo_ref[...], preferred_element_type=F32)
              + jnp.dot(lo, whi_ref[...], preferred_element_type=F32)) + b_ref[...]
    tm = logits.shape[0]
    lane = lax.broadcasted_iota(jnp.int32, (tm, LANES), 1)
    big = jnp.int32(LANES)

    def first_argmax(v):
        m = jnp.max(v, axis=-1, keepdims=True)
        idx = jnp.min(jnp.where(v == m, lane, big), axis=-1, keepdims=True)
        return m, idx

    is_group = lane < N_GROUPS
    gl = jnp.where(is_group, logits, NEG_INF)
    gmax, gsel = first_argmax(gl)
    p_g = 1.0 / jnp.sum(jnp.where(is_group, jnp.exp(gl - gmax), 0.0), axis=-1, keepdims=True)
    e_lo = N_GROUPS + gsel * EXPERTS_PER_GROUP
    in_group = (lane >= e_lo) & (lane < e_lo + EXPERTS_PER_GROUP)
    el = jnp.where(in_group, logits, NEG_INF)
    t1, i1 = first_argmax(el)
    el2 = jnp.where(lane == i1, NEG_INF, el)
    t2, i2 = first_argmax(el2)
    e2 = jnp.exp(t2 - t1)
    w1 = p_g / (1.0 + e2)
    w2 = p_g * e2 / (1.0 + e2)
    id1 = (i1 - N_GROUPS).astype(F32)
    id2 = (i2 - N_GROUPS).astype(F32)
    r_ref[...] = jnp.where(lane == 0, id1, jnp.where(lane == 1, id2,
                           jnp.where(lane == 2, w1, jnp.where(lane == 3, w2, 0.0))))


def _router(x2, g_ffn, shift2, scale2, w_gr, b_gr, w_er, b_er, seq, tm=1024):
    t, d = x2.shape
    nb = shift2.shape[0]
    pad = LANES - N_GROUPS - N_EXPERTS
    w = jnp.concatenate([w_gr, w_er, jnp.zeros((d, pad), F32)], axis=1)
    bias = jnp.concatenate([b_gr, b_er, jnp.zeros((pad,), F32)]).reshape(1, LANES)
    whi = w.astype(BF16)
    wlo = (w - whi.astype(F32)).astype(BF16)
    steps_per_seq = seq // tm
    bvec = lambda i: (i // steps_per_seq, 0, 0)
    return pl.pallas_call(
        _router_kernel,
        out_shape=(jax.ShapeDtypeStruct((t, PACK_SUB, LANES), jnp.uint32),
                   jax.ShapeDtypeStruct((t, LANES), F32)),
        grid=(t // tm,),
        in_specs=[
            pl.BlockSpec((tm, d), lambda i: (i, 0)),
            pl.BlockSpec((1, d), lambda i: (0, 0)),
            pl.BlockSpec((1, 1, d), bvec),
            pl.BlockSpec((1, 1, d), bvec),
            pl.BlockSpec((d, LANES), lambda i: (0, 0)),
            pl.BlockSpec((d, LANES), lambda i: (0, 0)),
            pl.BlockSpec((1, LANES), lambda i: (0, 0)),
        ],
        out_specs=(pl.BlockSpec((tm, PACK_SUB, LANES), lambda i: (i, 0, 0)),
                   pl.BlockSpec((tm, LANES), lambda i: (i, 0))),
        compiler_params=_params(("arbitrary",)),
        name="router",
    )(x2, g_ffn.reshape(1, d), shift2.reshape(nb, 1, d), scale2.reshape(nb, 1, d), whi, wlo, bias)


def _final_kernel(x_ref, y0_ref, y1_ref, r_ref, gate_ref, g_ref, o_ref):
    w0 = r_ref[:, 2:3]
    w1 = r_ref[:, 3:4]
    ffn = w0 * _unpack_rows(y0_ref[...]) + w1 * _unpack_rows(y1_ref[...])
    x = x_ref[...] + gate_ref[0] * ffn
    ms = jnp.mean(x * x, axis=-1, keepdims=True)
    o_ref[...] = (x * lax.rsqrt(ms + RMS_EPS)) * g_ref[...]


def _final(x2, ys, route, gate2, g_final, seq, tm=1024):
    t, d = x2.shape
    nb = gate2.shape[0]
    steps_per_seq = seq // tm
    return pl.pallas_call(
        _final_kernel,
        out_shape=jax.ShapeDtypeStruct((t, d), F32),
        grid=(t // tm,),
        in_specs=[
            pl.BlockSpec((tm, d), lambda i: (i, 0)),
            pl.BlockSpec((tm, PACK_SUB, LANES), lambda i: (i, 0, 0)),
            pl.BlockSpec((tm, PACK_SUB, LANES), lambda i: (i + t // tm, 0, 0)),
            pl.BlockSpec((tm, LANES), lambda i: (i, 0)),
            pl.BlockSpec((1, 1, d), lambda i: (i // steps_per_seq, 0, 0)),
            pl.BlockSpec((1, d), lambda i: (0, 0)),
        ],
        out_specs=pl.BlockSpec((tm, d), lambda i: (i, 0)),
        compiler_params=_params(("arbitrary",)),
        name="final",
    )(x2, ys, ys, route, gate2.reshape(nb, 1, d), g_final.reshape(1, d))


MOE_ROWS = 256
ROW_BUFS = 3
BLK_LEAD = 1
BLK_TRAIL = 2
WEIGHT_DMA_PRIORITY = 1


def _moe_kernel(blk_e, blk_ord, exp_list, n_active, blk_n, blk_i0, src_tok, dst_row,
                h_hbm, wgu_hbm, wd_hbm, ys_hbm,
                h0, h1, h2, o0, o1, o2, wgu_st, wd_st, wgu_bf, wd_bf, gsem, ssem, wsem):
    b = pl.program_id(0)
    tm = h0.shape[0]
    dump0 = ys_hbm.shape[0] - ROW_BUFS * tm
    hbufs, obufs = (h0, h1, h2), (o0, o1, o2)

    def count(blk):
        return blk_n[blk + BLK_LEAD]

    def first(blk):
        return blk_i0[blk + BLK_LEAD]

    def gather_row(i0, j, s):
        return pltpu.make_async_copy(h_hbm.at[src_tok[i0 + j]], hbufs[s].at[j], gsem.at[s])

    def scatter_row(i0, n_rows, j, s):
        row = jnp.where(j < n_rows, dst_row[i0 + j], dump0 + s * tm + j)
        return pltpu.make_async_copy(obufs[s].at[j], ys_hbm.at[row], ssem.at[s])

    def wait_gather(s):
        pltpu.make_async_copy(h_hbm.at[pl.ds(0, tm)], hbufs[s], gsem.at[s]).wait()

    def wait_scatter(s):
        pltpu.make_async_copy(obufs[s], ys_hbm.at[pl.ds(0, tm)], ssem.at[s]).wait()

    def loop_rows(start_row):
        def body(j, carry):
            start_row(j)
            return carry
        lax.fori_loop(0, tm, body, 0, unroll=8)

    @pl.when(b == 0)
    def _():
        for s in range(ROW_BUFS):
            obufs[s][...] = jnp.zeros(obufs[s].shape, obufs[s].dtype)
            dump = pltpu.make_async_copy(obufs[s], ys_hbm.at[pl.ds(dump0 + s * tm, tm)], ssem.at[s])
            dump.start()
            dump.wait()
        for blk in range(ROW_BUFS - 1):
            i0 = first(blk)
            loop_rows(lambda j, i0=i0, blk=blk: gather_row(i0, j, blk).start())

    def weight_copies(ordinal, slot):
        e = exp_list[ordinal]
        return (pltpu.make_async_copy(wgu_hbm.at[e], wgu_st.at[slot], wsem.at[slot, 0]),
                pltpu.make_async_copy(wd_hbm.at[e], wd_st.at[slot], wsem.at[slot, 1]))

    @pl.when(b == 0)
    def _():
        for ordinal in range(2):
            @pl.when(ordinal < n_active[0])
            def _():
                for cp in weight_copies(ordinal, ordinal):
                    cp.start(priority=WEIGHT_DMA_PRIORITY)

    @pl.when(count(b) > 0)
    def _():
        prev_e = blk_e[jnp.maximum(b - 1, 0)]

        @pl.when(jnp.logical_or(b == 0, blk_e[b] != prev_e))
        def _():
            ordinal = blk_ord[b]
            slot = ordinal % 2
            for cp in weight_copies(ordinal, slot):
                cp.wait()
            wgu_bf[...] = wgu_st[slot].astype(BF16)
            wd_bf[...] = wd_st[slot].astype(BF16)

            @pl.when(ordinal + 2 < n_active[0])
            def _():
                for cp in weight_copies(ordinal + 2, slot):
                    cp.start(priority=WEIGHT_DMA_PRIORITY)

        def block_step(s):
            t = (s + ROW_BUFS - 1) % ROW_BUFS
            u = (s + 1) % ROW_BUFS
            wait_gather(s)

            @pl.when(b >= ROW_BUFS - 1)
            def _():
                wait_scatter(s)

            i0_next, i0_prev, n_prev = first(b + 2), first(b - 1), count(b - 1)
            for j in range(tm):
                gather_row(i0_next, j, t).start()
                scatter_row(i0_prev, n_prev, j, t).start()
            h = _unpack_rows(hbufs[s][...]).astype(BF16)
            gu = jnp.dot(h, wgu_bf[...], preferred_element_type=F32)
            ff = gu.shape[1] // 2
            gate, up = gu[:, :ff], gu[:, ff:]
            act = (gate * jax.nn.sigmoid(gate) * up).astype(BF16)
            obufs[s][...] = _pack_rows(jnp.dot(act, wd_bf[...], preferred_element_type=F32))

            @pl.when(count(b + 1) == 0)
            def _():
                i0, n_rows = first(b), count(b)
                loop_rows(lambda j: scatter_row(i0, n_rows, j, s).start())

                @pl.when(b >= 1)
                def _():
                    wait_scatter(u)
                wait_scatter(t)
                wait_scatter(s)
                wait_gather(u)
                wait_gather(t)

        for s in range(ROW_BUFS):
            pl.when(b % ROW_BUFS == s)(functools.partial(block_step, s))


def _moe_plan(route, tm, t):
    ids = route[:, 0:TOP_K].astype(jnp.int32)
    flat_e = ids.reshape(-1)
    n_assign = flat_e.shape[0]
    order = jnp.argsort(flat_e).astype(jnp.int32)
    counts = jnp.sum(flat_e[:, None] == jnp.arange(N_EXPERTS, dtype=jnp.int32)[None, :], axis=0,
                     dtype=jnp.int32)
    raw_start = jnp.cumsum(counts) - counts
    nblk = (counts + tm - 1) // tm
    blk_end = jnp.cumsum(nblk)
    n_blocks = -(-(n_assign + N_EXPERTS * (tm - 1)) // tm)
    bidx = jnp.arange(n_blocks, dtype=jnp.int32)
    total = blk_end[-1]
    e_of = jnp.sum(jnp.minimum(bidx, total - 1)[:, None] >= blk_end[None, :], axis=1, dtype=jnp.int32)
    e_of = jnp.minimum(e_of, N_EXPERTS - 1)
    j_in = bidx - (blk_end[e_of] - nblk[e_of])
    blk_i0 = raw_start[e_of] + j_in * tm
    blk_n = jnp.where(bidx < total, jnp.clip(counts[e_of] - j_in * tm, 0, tm), 0).astype(jnp.int32)
    blk_i0 = jnp.where(bidx < total, blk_i0, 0).astype(jnp.int32)
    lead, trail = jnp.zeros((BLK_LEAD,), jnp.int32), jnp.zeros((BLK_TRAIL,), jnp.int32)
    blk_n = jnp.concatenate([lead, blk_n, trail])
    blk_i0 = jnp.concatenate([lead, blk_i0, trail])
    tok, k = order // TOP_K, order % TOP_K
    spare = jnp.arange(tm, dtype=jnp.int32) % t
    src_tok = jnp.concatenate([tok, spare])
    dst_row = jnp.concatenate([k * t + tok, spare])
    active = counts > 0
    ordinal_of = (jnp.cumsum(active) - 1).astype(jnp.int32)
    blk_ord = ordinal_of[e_of]
    exp_list = jnp.argsort(jnp.logical_not(active), stable=True).astype(jnp.int32)
    n_active = jnp.sum(active, dtype=jnp.int32).reshape(1)
    return e_of, blk_ord, exp_list, n_active, blk_n, blk_i0, src_tok, dst_row, n_blocks


def _moe(h2p, route, w_gate_up, w_down, tm=MOE_ROWS):
    t = h2p.shape[0]
    d, ff2 = w_gate_up.shape[1], w_gate_up.shape[2]
    n_assign = t * TOP_K
    *tables, n_blocks = _moe_plan(route, tm, t)
    row_buf = pltpu.VMEM((tm, PACK_SUB, LANES), jnp.uint32)
    return pl.pallas_call(
        _moe_kernel,
        out_shape=jax.ShapeDtypeStruct((n_assign + ROW_BUFS * tm, PACK_SUB, LANES), jnp.uint32),
        grid_spec=pltpu.PrefetchScalarGridSpec(
            num_scalar_prefetch=len(tables),
            grid=(n_blocks,),
            in_specs=[pl.BlockSpec(memory_space=pl.ANY)] * 3,
            out_specs=pl.BlockSpec(memory_space=pl.ANY),
            scratch_shapes=[row_buf] * (2 * ROW_BUFS) + [
                            pltpu.VMEM((2, d, ff2), F32), pltpu.VMEM((2, ff2 // 2, d), F32),
                            pltpu.VMEM((d, ff2), BF16), pltpu.VMEM((ff2 // 2, d), BF16),
                            pltpu.SemaphoreType.DMA((ROW_BUFS,)), pltpu.SemaphoreType.DMA((ROW_BUFS,)),
                            pltpu.SemaphoreType.DMA((2, 2))],
        ),
        compiler_params=_params(("arbitrary",)),
        name="moe",
    )(*tables, h2p, w_gate_up, w_down)


def kernel(x, c, positions, w_ada, b_ada, g_mix, w_in, b_in, attn_sinks, conv_w, conv_b, conv_ln_g,
           conv_ln_b, w_out, b_out, g_ffn, w_group_router, b_group_router, w_expert_router,
           b_expert_router, w_gate_up, w_down, g_final):
    nb, seq, d = x.shape
    t = nb * seq
    mod = _ada(c, w_ada, b_ada)
    shift1, scale1, gate1, shift2, scale2, gate2 = [mod[:, i * d:(i + 1) * d] for i in range(6)]
    x2d = x.reshape(t, d)
    q, kv, ab = _inproj(x2d, positions.reshape(t, 1), g_mix, shift1, scale1, w_in, b_in, seq)
    attn = _attn(q, kv, attn_sinks, nb, seq)
    x2 = _mix(ab, attn, x2d, conv_w, conv_b, conv_ln_g, conv_ln_b, w_out, b_out, gate1, nb, seq)
    h2, route = _router(x2, g_ffn, shift2, scale2, w_group_router, b_group_router,
                        w_expert_router, b_expert_router, seq)
    ys = _moe(h2, route, w_gate_up, w_down)
    out = _final(x2, ys, route, gate2, g_final, seq)
    return out.reshape(nb, seq, d)
```

```python
import functools
import math

import jax
import jax.numpy as jnp
from jax import lax
from jax.experimental import pallas as pl
from jax.experimental.pallas import tpu as pltpu

F32 = jnp.float32
BF16 = jnp.bfloat16

HEAD_DIM = 64
N_Q_HEADS = 16
N_KV_HEADS = 2
ATTN_WIDTH = N_Q_HEADS * HEAD_DIM
KV_WIDTH = N_KV_HEADS * HEAD_DIM
CONV_SIZE = 31
WINDOW = 128
ROPE_THETA = 10000.0
N_GROUPS = 8
EXPERTS_PER_GROUP = 8
N_EXPERTS = N_GROUPS * EXPERTS_PER_GROUP
TOP_K = 2
EXPERT_FF = 512
RMS_EPS = 1e-6
LN_EPS = 1e-5
NEG_INF = -1e30
LOG2E = 1.4426950408889634

LANES = 128
SUBLANES = 8
VMEM_LIMIT = 56 * 1024 * 1024


def _params(semantics, vmem=VMEM_LIMIT):
    return pltpu.CompilerParams(dimension_semantics=semantics, vmem_limit_bytes=vmem)


def _ada_kernel(cb_ref, w_ref, b_ref, o_ref, s_ref):
    nb, d = cb_ref.shape[0], w_ref.shape[0]
    nl = w_ref.shape[1] // LANES
    cv = cb_ref[...]
    s_ref[...] = cv * jax.nn.sigmoid(cv)

    def body(k, acc):
        r = pl.multiple_of(k * SUBLANES, SUBLANES)
        w8 = w_ref[pl.ds(r, SUBLANES), :]
        new = []
        for b in range(nb):
            sv = s_ref[b, pl.ds(r, SUBLANES), :]
            for j in range(nl):
                new.append(acc[b * nl + j] + w8[:, j * LANES:(j + 1) * LANES] * sv)
        return tuple(new)

    init = tuple(jnp.zeros((SUBLANES, LANES), F32) for _ in range(nb * nl))
    acc = lax.fori_loop(0, d // SUBLANES, body, init, unroll=4)
    for b in range(nb):
        row = jnp.concatenate(
            [jnp.sum(acc[b * nl + j], axis=0, keepdims=True) for j in range(nl)], axis=1)
        o_ref[b:b + 1, :] = row + b_ref[...]


def _ada(c, w_ada, b_ada, tn=1024):
    nb, d = c.shape
    n = w_ada.shape[1]
    cb = jnp.broadcast_to(c[:, :, None], (nb, d, LANES))
    return pl.pallas_call(
        _ada_kernel,
        out_shape=jax.ShapeDtypeStruct((nb, n), F32),
        grid=(n // tn,),
        in_specs=[
            pl.BlockSpec((nb, d, LANES), lambda j: (0, 0, 0)),
            pl.BlockSpec((d, tn), lambda j: (0, j)),
            pl.BlockSpec((1, tn), lambda j: (0, j)),
        ],
        out_specs=pl.BlockSpec((nb, tn), lambda j: (0, j)),
        scratch_shapes=[pltpu.VMEM((nb, d, LANES), F32)],
        compiler_params=_params(("arbitrary",)),
        name="ada",
    )(cb, w_ada, b_ada.reshape(1, n))


def _rmsnorm_mod(x, g, shift, scale):
    ms = jnp.mean(x * x, axis=-1, keepdims=True)
    h = (x * lax.rsqrt(ms + RMS_EPS)) * g
    return h * (1.0 + scale) + shift


def _inproj_kernel(x_ref, pos_ref, g_ref, sh_ref, sc_ref, w_ref, b_ref, invf_ref, sgn_ref, cj_ref, sj_ref,
                   q_ref, kv_ref, ab_ref):
    h = _rmsnorm_mod(x_ref[...], g_ref[...], sh_ref[0], sc_ref[0]).astype(BF16)

    a0 = pos_ref[0:1, :].astype(F32) * invf_ref[...]
    c0, s0 = jnp.cos(a0), jnp.sin(a0)
    cj, sj = cj_ref[...], sj_ref[...]
    cosv = c0 * cj - s0 * sj
    sinv = (s0 * cj + c0 * sj) * sgn_ref[...]
    lane = lax.broadcasted_iota(jnp.int32, (1, LANES), 1)
    first_half = (lane % HEAD_DIM) < (HEAD_DIM // 2)

    def rope(t):
        rot = jnp.where(first_half, pltpu.roll(t, LANES - HEAD_DIM // 2, 1),
                        pltpu.roll(t, HEAD_DIM // 2, 1))
        return t * cosv + rot * sinv

    def proj(c0, width):
        return (jnp.dot(h, w_ref[:, c0:c0 + width], preferred_element_type=F32)
                + b_ref[:, c0:c0 + width])

    qscale = LOG2E / math.sqrt(HEAD_DIM)
    step = 512
    for c0 in range(0, ATTN_WIDTH, step):
        z = proj(c0, step)
        for j in range(step // LANES):
            q_ref[:, c0 + j * LANES:c0 + (j + 1) * LANES] = (
                rope(z[:, j * LANES:(j + 1) * LANES]) * qscale).astype(BF16)
    z = proj(ATTN_WIDTH, 2 * KV_WIDTH)
    k = rope(z[:, :KV_WIDTH])
    v = z[:, KV_WIDTH:]
    kv_ref[:, 0 * LANES:1 * LANES] = k.astype(BF16)
    kv_ref[:, 1 * LANES:2 * LANES] = pltpu.roll(k, HEAD_DIM, 1).astype(BF16)
    kv_ref[:, 2 * LANES:3 * LANES] = v.astype(BF16)
    kv_ref[:, 3 * LANES:4 * LANES] = pltpu.roll(v, HEAD_DIM, 1).astype(BF16)
    base = ATTN_WIDTH + 2 * KV_WIDTH
    for c0 in range(0, ab_ref.shape[1], step):
        ab_ref[:, c0:c0 + step] = proj(base + c0, step).astype(BF16)


def _inproj(x2d, pos2d, g_mix, shift1, scale1, w_in, b_in, seq, tm=512):
    t, d = x2d.shape
    n = w_in.shape[1]
    nb = shift1.shape[0]
    half = HEAD_DIM // 2
    inv_freq = ROPE_THETA ** (-jnp.arange(half, dtype=F32) * 2.0 / HEAD_DIM)
    invf = jnp.tile(inv_freq, LANES // half).reshape(1, LANES)
    sgn = jnp.tile(jnp.concatenate([-jnp.ones((half,), F32), jnp.ones((half,), F32)]),
                   LANES // HEAD_DIM).reshape(1, LANES)
    steps_per_seq = seq // tm
    conv_w2 = n - ATTN_WIDTH - 2 * KV_WIDTH
    bvec = lambda i: (i // steps_per_seq, 0, 0)
    ang_j = jnp.arange(tm, dtype=F32)[:, None] * invf
    cj, sj = jnp.cos(ang_j), jnp.sin(ang_j)
    return pl.pallas_call(
        _inproj_kernel,
        out_shape=(jax.ShapeDtypeStruct((t, ATTN_WIDTH), BF16),
                   jax.ShapeDtypeStruct((t, 4 * KV_WIDTH), BF16),
                   jax.ShapeDtypeStruct((t, conv_w2), BF16)),
        grid=(t // tm,),
        in_specs=[
            pl.BlockSpec((tm, d), lambda i: (i, 0)),
            pl.BlockSpec((tm, 1), lambda i: (i, 0)),
            pl.BlockSpec((1, d), lambda i: (0, 0)),
            pl.BlockSpec((1, 1, d), bvec),
            pl.BlockSpec((1, 1, d), bvec),
            pl.BlockSpec((d, n), lambda i: (0, 0)),
            pl.BlockSpec((1, n), lambda i: (0, 0)),
            pl.BlockSpec((1, LANES), lambda i: (0, 0)),
            pl.BlockSpec((1, LANES), lambda i: (0, 0)),
            pl.BlockSpec((tm, LANES), lambda i: (0, 0)),
            pl.BlockSpec((tm, LANES), lambda i: (0, 0)),
        ],
        out_specs=(pl.BlockSpec((tm, ATTN_WIDTH), lambda i: (i, 0)),
                   pl.BlockSpec((tm, 4 * KV_WIDTH), lambda i: (i, 0)),
                   pl.BlockSpec((tm, conv_w2), lambda i: (i, 0))),
        compiler_params=_params(("arbitrary",)),
        name="inproj",
    )(x2d, pos2d, g_mix.reshape(1, d), shift1.reshape(nb, 1, d), scale1.reshape(nb, 1, d),
      w_in.astype(BF16), b_in.reshape(1, n), invf, sgn, cj, sj)


def _attn_kernel(sink_ref, q_ref, kvm_ref, kvh_ref, o_ref):
    i = pl.program_id(1)
    tq = q_ref.shape[0]
    blk = WINDOW
    heads_per_group = N_Q_HEADS // N_KV_HEADS
    pairs = heads_per_group // 2
    row = lax.broadcasted_iota(jnp.int32, (blk, 2 * blk), 0)
    col = lax.broadcasted_iota(jnp.int32, (blk, 2 * blk), 1)
    rel = row + blk - col
    band = (rel >= 0) & (rel < WINDOW)
    lane = lax.broadcasted_iota(jnp.int32, (1, LANES), 1)
    lo = lane < HEAD_DIM
    zero = jnp.zeros((), BF16)

    for j in range(tq // blk):
        if j == 0:
            kv = jnp.concatenate([kvh_ref[...], kvm_ref[0:blk, :]], axis=0)
            valid = band & ((col >= blk) | (i > 0))
        else:
            kv = kvm_ref[(j - 1) * blk:(j + 1) * blk, :]
            valid = band
        k_nat, k_swp = kv[:, 0:LANES], kv[:, LANES:2 * LANES]
        v_nat, v_swp = kv[:, 2 * LANES:3 * LANES], kv[:, 3 * LANES:4 * LANES]
        for g in range(N_KV_HEADS):
            src_lo_k, src_hi_k = (k_nat, k_swp) if g == 0 else (k_swp, k_nat)
            src_lo_v, src_hi_v = (v_nat, v_swp) if g == 0 else (v_swp, v_nat)
            k_even = jnp.where(lo, src_lo_k, zero)
            k_odd = jnp.where(lo, zero, src_hi_k)
            v_even = jnp.where(lo, src_lo_v, zero)
            v_odd = jnp.where(lo, zero, src_hi_v)
            qs = jnp.concatenate(
                [q_ref[j * blk:(j + 1) * blk, (g * pairs + p) * LANES:(g * pairs + p + 1) * LANES]
                 for p in range(pairs)], axis=0)
            nt = (((1,), (1,)), ((), ()))
            s_even = lax.dot_general(qs, k_even, nt, preferred_element_type=F32)
            s_odd = lax.dot_general(qs, k_odd, nt, preferred_element_type=F32)
            probs, rinv = [], []
            for s_all, parity in ((s_even, 0), (s_odd, 1)):
                ps, rs = [], []
                for p in range(pairs):
                    sink = sink_ref[g * heads_per_group + 2 * p + parity] * LOG2E
                    s = jnp.where(valid, s_all[p * blk:(p + 1) * blk, :], NEG_INF)
                    m = jnp.maximum(jnp.max(s, axis=-1, keepdims=True), sink)
                    e = jnp.exp2(s - m)
                    den = jnp.sum(e, axis=-1, keepdims=True) + jnp.exp2(sink - m)
                    ps.append(e.astype(BF16))
                    rs.append(1.0 / den)
                probs.append(jnp.concatenate(ps, axis=0))
                rinv.append(rs)
            o = (jnp.dot(probs[0], v_even, preferred_element_type=F32)
                 + jnp.dot(probs[1], v_odd, preferred_element_type=F32))
            for p in range(pairs):
                scale = jnp.where(lo, rinv[0][p], rinv[1][p])
                c = (g * pairs + p) * LANES
                o_ref[j * blk:(j + 1) * blk, c:c + LANES] = (
                    o[p * blk:(p + 1) * blk, :] * scale).astype(BF16)


def _attn(q, kv, sinks, nb, seq, tq=512):
    blk = WINDOW
    q3 = q.reshape(nb, seq, ATTN_WIDTH)
    kv3 = kv.reshape(nb, seq, 4 * KV_WIDTH)
    sub = tq // blk
    out = pl.pallas_call(
        _attn_kernel,
        out_shape=jax.ShapeDtypeStruct((nb, seq, ATTN_WIDTH), BF16),
        grid=(nb, seq // tq),
        in_specs=[
            pl.BlockSpec(memory_space=pltpu.SMEM),
            pl.BlockSpec((None, tq, ATTN_WIDTH), lambda b, i: (b, i, 0)),
            pl.BlockSpec((None, tq, 4 * KV_WIDTH), lambda b, i: (b, i, 0)),
            pl.BlockSpec((None, blk, 4 * KV_WIDTH), lambda b, i: (b, jnp.maximum(i * sub - 1, 0), 0)),
        ],
        out_specs=pl.BlockSpec((None, tq, ATTN_WIDTH), lambda b, i: (b, i, 0)),
        compiler_params=_params(("arbitrary", "arbitrary")),
        name="attn",
    )(sinks.astype(F32), q3, kv3, kv3)
    return out.reshape(nb * seq, ATTN_WIDTH)


CONV_HALO = 32
CONV_ROWS = 256


def _mix_kernel(ab_ref, abh_ref, attn_ref, x_ref, wc_ref, cb_ref, lg_ref, lb_ref, wo_ref, bo_ref,
                gate_ref, o_ref, us_ref, y_ref):
    i = pl.program_id(1)
    tt = x_ref.shape[0]
    cw = y_ref.shape[1]

    def glu(ab):
        a = ab[:, :cw].astype(F32)
        b = ab[:, cw:].astype(F32)
        return a * jax.nn.sigmoid(b)

    halo = glu(abh_ref[...])
    us_ref[0, 0:CONV_HALO, :] = jnp.where(i > 0, halo, 0.0)
    us_ref[0, CONV_HALO:, :] = glu(ab_ref[...])
    n = tt + CONV_HALO - SUBLANES
    for s in range(1, SUBLANES):
        us_ref[s, 0:n, :] = us_ref[0, s:s + n, :]

    off = CONV_HALO - (CONV_SIZE - 1)

    def conv_lanes(c, carry):
        l0 = pl.multiple_of(c * LANES, LANES)
        for r0 in range(0, tt, CONV_ROWS):
            acc = jnp.zeros((CONV_ROWS, LANES), F32)
            for k in range(CONV_SIZE):
                o = off + k
                a0 = (o // SUBLANES) * SUBLANES + r0
                acc = acc + (us_ref[o % SUBLANES, a0:a0 + CONV_ROWS, pl.ds(l0, LANES)]
                             * wc_ref[k:k + 1, pl.ds(l0, LANES)])
            y_ref[r0:r0 + CONV_ROWS, pl.ds(l0, LANES)] = acc
        return carry

    lax.fori_loop(0, cw // LANES, conv_lanes, 0)

    y = y_ref[...] + cb_ref[...]
    mu = jnp.mean(y, axis=-1, keepdims=True)
    yc = y - mu
    var = jnp.mean(yc * yc, axis=-1, keepdims=True)
    yn = yc * lax.rsqrt(var + LN_EPS) * lg_ref[...] + lb_ref[...]
    conv = (yn * jax.nn.sigmoid(yn)).astype(BF16)

    aw = attn_ref.shape[1]
    mixed = (jnp.dot(attn_ref[...], wo_ref[0:aw, :], preferred_element_type=F32)
             + jnp.dot(conv, wo_ref[aw:, :], preferred_element_type=F32) + bo_ref[...])
    o_ref[...] = x_ref[...] + gate_ref[0] * mixed


def _mix(ab, attn, x2d, conv_w, conv_b, ln_g, ln_b, w_out, b_out, gate1, nb, seq, tt=512):
    t, d = x2d.shape
    cw = conv_w.shape[1]
    aw = attn.shape[1]
    ab3 = ab.reshape(nb, seq, 2 * cw)
    attn3 = attn.reshape(nb, seq, aw)
    x3 = x2d.reshape(nb, seq, d)
    hsub = tt // CONV_HALO
    out = pl.pallas_call(
        _mix_kernel,
        out_shape=jax.ShapeDtypeStruct((nb, seq, d), F32),
        grid=(nb, seq // tt),
        in_specs=[
            pl.BlockSpec((None, tt, 2 * cw), lambda b, i: (b, i, 0)),
            pl.BlockSpec((None, CONV_HALO, 2 * cw), lambda b, i: (b, jnp.maximum(i * hsub - 1, 0), 0)),
            pl.BlockSpec((None, tt, aw), lambda b, i: (b, i, 0)),
            pl.BlockSpec((None, tt, d), lambda b, i: (b, i, 0)),
            pl.BlockSpec((CONV_SIZE, cw), lambda b, i: (0, 0)),
            pl.BlockSpec((1, cw), lambda b, i: (0, 0)),
            pl.BlockSpec((1, cw), lambda b, i: (0, 0)),
            pl.BlockSpec((1, cw), lambda b, i: (0, 0)),
            pl.BlockSpec((d, d), lambda b, i: (0, 0), pipeline_mode=pl.Buffered(1)),
            pl.BlockSpec((1, d), lambda b, i: (0, 0)),
            pl.BlockSpec((1, 1, d), lambda b, i: (b, 0, 0)),
        ],
        out_specs=pl.BlockSpec((None, tt, d), lambda b, i: (b, i, 0)),
        scratch_shapes=[pltpu.VMEM((SUBLANES, CONV_HALO + tt, cw), F32), pltpu.VMEM((tt, cw), F32)],
        compiler_params=_params(("arbitrary", "arbitrary")),
        name="mix",
    )(ab3, ab3, attn3, x3, conv_w, conv_b.reshape(1, cw), ln_g.reshape(1, cw), ln_b.reshape(1, cw),
      w_out.astype(BF16), b_out.reshape(1, d), gate1.reshape(nb, 1, d))
    return out.reshape(t, d)


PACK_SUB = SUBLANES
HI_MASK = 0xFFFF0000


def _pack_rows(v):
    return _pack_rounded(v.astype(BF16).astype(F32))


def _pack_rounded(v):
    half = v.shape[1] // 2
    assert half == PACK_SUB * LANES
    bits = lax.bitcast_convert_type(v, jnp.uint32)
    word = (bits[:, :half] >> 16) | (bits[:, half:] & jnp.uint32(HI_MASK))
    planes = jnp.stack([word[:, s * LANES:(s + 1) * LANES] for s in range(PACK_SUB)], axis=0)
    return pltpu.einshape("smr->msr", planes)


def _unpack_rows(p):
    planes = pltpu.einshape("msr->smr", p)
    lo = [lax.bitcast_convert_type(planes[s] << 16, F32) for s in range(PACK_SUB)]
    hi = [lax.bitcast_convert_type(planes[s] & jnp.uint32(HI_MASK), F32) for s in range(PACK_SUB)]
    return jnp.concatenate(lo + hi, axis=1)


def _router_kernel(x_ref, g_ref, sh_ref, sc_ref, whi_ref, wlo_ref, b_ref, h_ref, r_ref):
    h = _rmsnorm_mod(x_ref[...], g_ref[...], sh_ref[0], sc_ref[0])
    hi = h.astype(BF16)
    hi_f32 = hi.astype(F32)
    h_ref[...] = _pack_rounded(hi_f32)
    lo = (h - hi_f32).astype(BF16)
    logits = (jnp.dot(hi, whi_ref[...], preferred_element_type=F32)
              + jnp.dot(hi, wlo_ref[...], preferred_element_type=F32)
              + jnp.dot(lo, whi_ref[...], preferred_element_type=F32)) + b_ref[...]
    tm = logits.shape[0]
    lane = lax.broadcasted_iota(jnp.int32, (tm, LANES), 1)
    big = jnp.int32(LANES)

    def first_argmax(v):
        m = jnp.max(v, axis=-1, keepdims=True)
        idx = jnp.min(jnp.where(v == m, lane, big), axis=-1, keepdims=True)
        return m, idx

    is_group = lane < N_GROUPS
    gl = jnp.where(is_group, logits, NEG_INF)
    gmax, gsel = first_argmax(gl)
    p_g = 1.0 / jnp.sum(jnp.where(is_group, jnp.exp(gl - gmax), 0.0), axis=-1, keepdims=True)
    e_lo = N_GROUPS + gsel * EXPERTS_PER_GROUP
    in_group = (lane >= e_lo) & (lane < e_lo + EXPERTS_PER_GROUP)
    el = jnp.where(in_group, logits, NEG_INF)
    t1, i1 = first_argmax(el)
    el2 = jnp.where(lane == i1, NEG_INF, el)
    t2, i2 = first_argmax(el2)
    e2 = jnp.exp(t2 - t1)
    w1 = p_g / (1.0 + e2)
    w2 = p_g * e2 / (1.0 + e2)
    id1 = (i1 - N_GROUPS).astype(F32)
    id2 = (i2 - N_GROUPS).astype(F32)
    r_ref[...] = jnp.where(lane == 0, id1, jnp.where(lane == 1, id2,
                           jnp.where(lane == 2, w1, jnp.where(lane == 3, w2, 0.0))))


def _router(x2, g_ffn, shift2, scale2, w_gr, b_gr, w_er, b_er, seq, tm=1024):
    t, d = x2.shape
    nb = shift2.shape[0]
    pad = LANES - N_GROUPS - N_EXPERTS
    w = jnp.concatenate([w_gr, w_er, jnp.zeros((d, pad), F32)], axis=1)
    bias = jnp.concatenate([b_gr, b_er, jnp.zeros((pad,), F32)]).reshape(1, LANES)
    whi = w.astype(BF16)
    wlo = (w - whi.astype(F32)).astype(BF16)
    steps_per_seq = seq // tm
    bvec = lambda i: (i // steps_per_seq, 0, 0)
    return pl.pallas_call(
        _router_kernel,
        out_shape=(jax.ShapeDtypeStruct((t, PACK_SUB, LANES), jnp.uint32),
                   jax.ShapeDtypeStruct((t, LANES), F32)),
        grid=(t // tm,),
        in_specs=[
            pl.BlockSpec((tm, d), lambda i: (i, 0)),
            pl.BlockSpec((1, d), lambda i: (0, 0)),
            pl.BlockSpec((1, 1, d), bvec),
            pl.BlockSpec((1, 1, d), bvec),
            pl.BlockSpec((d, LANES), lambda i: (0, 0)),
            pl.BlockSpec((d, LANES), lambda i: (0, 0)),
            pl.BlockSpec((1, LANES), lambda i: (0, 0)),
        ],
        out_specs=(pl.BlockSpec((tm, PACK_SUB, LANES), lambda i: (i, 0, 0)),
                   pl.BlockSpec((tm, LANES), lambda i: (i, 0))),
        compiler_params=_params(("arbitrary",)),
        name="router",
    )(x2, g_ffn.reshape(1, d), shift2.reshape(nb, 1, d), scale2.reshape(nb, 1, d), whi, wlo, bias)


def _final_kernel(x_ref, y0_ref, y1_ref, r_ref, gate_ref, g_ref, o_ref):
    w0 = r_ref[:, 2:3]
    w1 = r_ref[:, 3:4]
    ffn = w0 * _unpack_rows(y0_ref[...]) + w1 * _unpack_rows(y1_ref[...])
    x = x_ref[...] + gate_ref[0] * ffn
    ms = jnp.mean(x * x, axis=-1, keepdims=True)
    o_ref[...] = (x * lax.rsqrt(ms + RMS_EPS)) * g_ref[...]


def _final(x2, ys, route, gate2, g_final, seq, tm=1024):
    t, d = x2.shape
    nb = gate2.shape[0]
    steps_per_seq = seq // tm
    return pl.pallas_call(
        _final_kernel,
        out_shape=jax.ShapeDtypeStruct((t, d), F32),
        grid=(t // tm,),
        in_specs=[
            pl.BlockSpec((tm, d), lambda i: (i, 0)),
            pl.BlockSpec((tm, PACK_SUB, LANES), lambda i: (i, 0, 0)),
            pl.BlockSpec((tm, PACK_SUB, LANES), lambda i: (i + t // tm, 0, 0)),
            pl.BlockSpec((tm, LANES), lambda i: (i, 0)),
            pl.BlockSpec((1, 1, d), lambda i: (i // steps_per_seq, 0, 0)),
            pl.BlockSpec((1, d), lambda i: (0, 0)),
        ],
        out_specs=pl.BlockSpec((tm, d), lambda i: (i, 0)),
        compiler_params=_params(("arbitrary",)),
        name="final",
    )(x2, ys, ys, route, gate2.reshape(nb, 1, d), g_final.reshape(1, d))


MOE_ROWS = 256
ROW_BUFS = 3
BLK_LEAD = 1
BLK_TRAIL = 2
WEIGHT_DMA_PRIORITY = 1


def _moe_kernel(blk_e, blk_ord, exp_list, n_active, blk_n, blk_i0, src_tok, dst_row,
                h_hbm, wgu_hbm, wd_hbm, ys_hbm,
                h0, h1, h2, o0, o1, o2, wgu_st, wd_st, wgu_bf, wd_bf, gsem, ssem, wsem):
    b = pl.program_id(0)
    tm = h0.shape[0]
    dump0 = ys_hbm.shape[0] - ROW_BUFS * tm
    hbufs, obufs = (h0, h1, h2), (o0, o1, o2)

    def count(blk):
        return blk_n[blk + BLK_LEAD]

    def first(blk):
        return blk_i0[blk + BLK_LEAD]

    def gather_row(i0, j, s):
        return pltpu.make_async_copy(h_hbm.at[src_tok[i0 + j]], hbufs[s].at[j], gsem.at[s])

    def scatter_row(i0, n_rows, j, s):
        row = jnp.where(j < n_rows, dst_row[i0 + j], dump0 + s * tm + j)
        return pltpu.make_async_copy(obufs[s].at[j], ys_hbm.at[row], ssem.at[s])

    def wait_gather(s):
        pltpu.make_async_copy(h_hbm.at[pl.ds(0, tm)], hbufs[s], gsem.at[s]).wait()

    def wait_scatter(s):
        pltpu.make_async_copy(obufs[s], ys_hbm.at[pl.ds(0, tm)], ssem.at[s]).wait()

    def loop_rows(start_row):
        def body(j, carry):
            start_row(j)
            return carry
        lax.fori_loop(0, tm, body, 0, unroll=8)

    @pl.when(b == 0)
    def _():
        for s in range(ROW_BUFS):
            obufs[s][...] = jnp.zeros(obufs[s].shape, obufs[s].dtype)
            dump = pltpu.make_async_copy(obufs[s], ys_hbm.at[pl.ds(dump0 + s * tm, tm)], ssem.at[s])
            dump.start()
            dump.wait()
        for blk in range(ROW_BUFS - 1):
            i0 = first(blk)
            loop_rows(lambda j, i0=i0, blk=blk: gather_row(i0, j, blk).start())

    def weight_copies(ordinal, slot):
        e = exp_list[ordinal]
        return (pltpu.make_async_copy(wgu_hbm.at[e], wgu_st.at[slot], wsem.at[slot, 0]),
                pltpu.make_async_copy(wd_hbm.at[e], wd_st.at[slot], wsem.at[slot, 1]))

    @pl.when(b == 0)
    def _():
        for ordinal in range(2):
            @pl.when(ordinal < n_active[0])
            def _():
                for cp in weight_copies(ordinal, ordinal):
                    cp.start(priority=WEIGHT_DMA_PRIORITY)

    @pl.when(count(b) > 0)
    def _():
        prev_e = blk_e[jnp.maximum(b - 1, 0)]

        @pl.when(jnp.logical_or(b == 0, blk_e[b] != prev_e))
        def _():
            ordinal = blk_ord[b]
            slot = ordinal % 2
            for cp in weight_copies(ordinal, slot):
                cp.wait()
            wgu_bf[...] = wgu_st[slot].astype(BF16)
            wd_bf[...] = wd_st[slot].astype(BF16)

            @pl.when(ordinal + 2 < n_active[0])
            def _():
                for cp in weight_copies(ordinal + 2, slot):
                    cp.start(priority=WEIGHT_DMA_PRIORITY)

        def block_step(s):
            t = (s + ROW_BUFS - 1) % ROW_BUFS
            u = (s + 1) % ROW_BUFS
            wait_gather(s)

            @pl.when(b >= ROW_BUFS - 1)
            def _():
                wait_scatter(s)

            i0_next, i0_prev, n_prev = first(b + 2), first(b - 1), count(b - 1)
            for j in range(tm):
                gather_row(i0_next, j, t).start()
                scatter_row(i0_prev, n_prev, j, t).start()
            h = _unpack_rows(hbufs[s][...]).astype(BF16)
            gu = jnp.dot(h, wgu_bf[...], preferred_element_type=F32)
            ff = gu.shape[1] // 2
            gate, up = gu[:, :ff], gu[:, ff:]
            act = (gate * jax.nn.sigmoid(gate) * up).astype(BF16)
            obufs[s][...] = _pack_rows(jnp.dot(act, wd_bf[...], preferred_element_type=F32))

            @pl.when(count(b + 1) == 0)
            def _():
                i0, n_rows = first(b), count(b)
                loop_rows(lambda j: scatter_row(i0, n_rows, j, s).start())

                @pl.when(b >= 1)
                def _():
                    wait_scatter(u)
                wait_scatter(t)
                wait_scatter(s)
                wait_gather(u)
                wait_gather(t)

        for s in range(ROW_BUFS):
            pl.when(b % ROW_BUFS == s)(functools.partial(block_step, s))


def _moe_plan(route, tm, t):
    ids = route[:, 0:TOP_K].astype(jnp.int32)
    flat_e = ids.reshape(-1)
    n_assign = flat_e.shape[0]
    order = jnp.argsort(flat_e).astype(jnp.int32)
    counts = jnp.sum(flat_e[:, None] == jnp.arange(N_EXPERTS, dtype=jnp.int32)[None, :], axis=0,
                     dtype=jnp.int32)
    raw_start = jnp.cumsum(counts) - counts
    nblk = (counts + tm - 1) // tm
    blk_end = jnp.cumsum(nblk)
    n_blocks = -(-(n_assign + N_EXPERTS * (tm - 1)) // tm)
    bidx = jnp.arange(n_blocks, dtype=jnp.int32)
    total = blk_end[-1]
    e_of = jnp.sum(jnp.minimum(bidx, total - 1)[:, None] >= blk_end[None, :], axis=1, dtype=jnp.int32)
    e_of = jnp.minimum(e_of, N_EXPERTS - 1)
    j_in = bidx - (blk_end[e_of] - nblk[e_of])
    blk_i0 = raw_start[e_of] + j_in * tm
    blk_n = jnp.where(bidx < total, jnp.clip(counts[e_of] - j_in * tm, 0, tm), 0).astype(jnp.int32)
    blk_i0 = jnp.where(bidx < total, blk_i0, 0).astype(jnp.int32)
    lead, trail = jnp.zeros((BLK_LEAD,), jnp.int32), jnp.zeros((BLK_TRAIL,), jnp.int32)
    blk_n = jnp.concatenate([lead, blk_n, trail])
    blk_i0 = jnp.concatenate([lead, blk_i0, trail])
    tok, k = order // TOP_K, order % TOP_K
    spare = jnp.arange(tm, dtype=jnp.int32) % t
    src_tok = jnp.concatenate([tok, spare])
    dst_row = jnp.concatenate([k * t + tok, spare])
    active = counts > 0
    ordinal_of = (jnp.cumsum(active) - 1).astype(jnp.int32)
    blk_ord = ordinal_of[e_of]
    exp_list = jnp.argsort(jnp.logical_not(active), stable=True).astype(jnp.int32)
    n_active = jnp.sum(active, dtype=jnp.int32).reshape(1)
    return e_of, blk_ord, exp_list, n_active, blk_n, blk_i0, src_tok, dst_row, n_blocks


def _moe(h2p, route, w_gate_up, w_down, tm=MOE_ROWS):
    t = h2p.shape[0]
    d, ff2 = w_gate_up.shape[1], w_gate_up.shape[2]
    n_assign = t * TOP_K
    *tables, n_blocks = _moe_plan(route, tm, t)
    row_buf = pltpu.VMEM((tm, PACK_SUB, LANES), jnp.uint32)
    return pl.pallas_call(
        _moe_kernel,
        out_shape=jax.ShapeDtypeStruct((n_assign + ROW_BUFS * tm, PACK_SUB, LANES), jnp.uint32),
        grid_spec=pltpu.PrefetchScalarGridSpec(
            num_scalar_prefetch=len(tables),
            grid=(n_blocks,),
            in_specs=[pl.BlockSpec(memory_space=pl.ANY)] * 3,
            out_specs=pl.BlockSpec(memory_space=pl.ANY),
            scratch_shapes=[row_buf] * (2 * ROW_BUFS) + [
                            pltpu.VMEM((2, d, ff2), F32), pltpu.VMEM((2, ff2 // 2, d), F32),
                            pltpu.VMEM((d, ff2), BF16), pltpu.VMEM((ff2 // 2, d), BF16),
                            pltpu.SemaphoreType.DMA((ROW_BUFS,)), pltpu.SemaphoreType.DMA((ROW_BUFS,)),
                            pltpu.SemaphoreType.DMA((2, 2))],
        ),
        compiler_params=_params(("arbitrary",)),
        name="moe",
    )(*tables, h2p, w_gate_up, w_down)


def kernel(x, c, positions, w_ada, b_ada, g_mix, w_in, b_in, attn_sinks, conv_w, conv_b, conv_ln_g,
           conv_ln_b, w_out, b_out, g_ffn, w_group_router, b_group_router, w_expert_router,
           b_expert_router, w_gate_up, w_down, g_final):
    nb, seq, d = x.shape
    t = nb * seq
    mod = _ada(c, w_ada, b_ada)
    shift1, scale1, gate1, shift2, scale2, gate2 = [mod[:, i * d:(i + 1) * d] for i in range(6)]
    x2d = x.reshape(t, d)
    q, kv, ab = _inproj(x2d, positions.reshape(t, 1), g_mix, shift1, scale1, w_in, b_in, seq)
    attn = _attn(q, kv, attn_sinks, nb, seq)
    x2 = _mix(ab, attn, x2d, conv_w, conv_b, conv_ln_g, conv_ln_b, w_out, b_out, gate1, nb, seq)
    h2, route = _router(x2, g_ffn, shift2, scale2, w_group_router, b_group_router,
                        w_expert_router, b_expert_router, seq)
    ys = _moe(h2, route, w_gate_up, w_down)
    out = _final(x2, ys, route, gate2, g_final, seq)
    return out.reshape(nb, seq, d)
```
